```python
import jax, jax.numpy as jnp
from jax import lax
import numpy as np

D_MODEL = 1024
BATCH = 2
SEQ = 16384
DEPTH = 4

GRID_W = 64
CTX_LEN = 256
N_MIXERS = 3
N_LRU = (DEPTH + 2) // 3
N_MLSTM = (DEPTH + 1) // 3
N_RWKV = DEPTH // 3

LRU_WIDTH = 1280
LRU_BLOCKS = 10
LRU_BS = LRU_WIDTH // LRU_BLOCKS
LRU_C = 8.0
CONV_W = 4

MLSTM_HEADS = 8
MLSTM_DK = 128
MLSTM_DV = 256
MLSTM_QK = MLSTM_HEADS * MLSTM_DK
MLSTM_WIDTH = MLSTM_HEADS * MLSTM_DV
MLSTM_CHUNK = 64

RWKV_HEAD = 64
RWKV_WIDTH = D_MODEL
RWKV_HEADS = RWKV_WIDTH // RWKV_HEAD
DECAY_RANK = 64
ICL_RANK = 64
GN_EPS = 64e-5
NORM_EPS = 1e-6

kernel_name = 'hybrid_rglru_mlstm_rwkv7_prefix_trunk'


def rmsnorm(x, g):
    xf = x.astype(jnp.float32)
    y = xf * lax.rsqrt(jnp.mean(xf * xf, axis=-1, keepdims=True) + NORM_EPS)
    return (y * g.astype(jnp.float32)).astype(x.dtype)


def ada_params(cond, w, b):
    m = jax.nn.silu(cond) @ w + b
    return jnp.split(m, 3, axis=-1)


def centred_dwconv(u, w, b):
    k = w.shape[0]
    left = k // 2
    t = u.shape[1]
    up = jnp.pad(u, ((0, 0), (left, k - 1 - left), (0, 0)))
    out = b
    for j in range(k):
        out = out + up[:, j:j + t] * w[j]
    return out


def rglru_coeffs(u, gate_w, gate_b, lam):
    bsz, t, e = u.shape
    uf = u.astype(jnp.float32)
    ub = uf.reshape(bsz, t, LRU_BLOCKS, LRU_BS)
    gates = jnp.einsum('btnj,dgnjk->dgbtnk', ub, gate_w.astype(jnp.float32)).reshape(2, 2, bsz, t, e)
    gates = jax.nn.sigmoid(gates + gate_b[:, :, None, None, :].astype(jnp.float32))
    r, i = gates[:, 0], gates[:, 1]
    log_a = -LRU_C * jax.nn.softplus(-lam.astype(jnp.float32))[:, None, None, :] * r
    a = jnp.exp(log_a)
    b = jnp.sqrt(-jnp.expm1(2.0 * log_a)) * (i * uf[None])
    return a, b


def linear_scan(a, b, h0, reverse):
    def step(h, ab):
        h = ab[0] * h + ab[1]
        return h, h
    h_last, hs = lax.scan(step, h0, (jnp.moveaxis(a, 1, 0), jnp.moveaxis(b, 1, 0)), reverse=reverse)
    return jnp.moveaxis(hs, 0, 1), h_last


def rglru_bidir(a, b, h0):
    hf, lf = linear_scan(a[0], b[0], h0[0], False)
    hb, lb = linear_scan(a[1], b[1], h0[1], True)
    return hf + hb, jnp.stack([lf, lb])


def lru_mixer(hc, hx, w_in, conv_w, conv_b, gate_w, gate_b, lam, w_out, with_ctx):
    def coeffs(h):
        u, z = jnp.split(h @ w_in, 2, axis=-1)
        a, b = rglru_coeffs(centred_dwconv(u, conv_w, conv_b), gate_w, gate_b, lam)
        return a, b, z
    ac, bc, zc = coeffs(hc)
    h0 = jnp.zeros((2, hc.shape[0], LRU_WIDTH), jnp.float32)
    yc, state_c = rglru_bidir(ac, bc, h0)
    ax, bx, zx = coeffs(hx)
    yx, _ = rglru_bidir(ax, bx, state_c)
    out_x = (yx.astype(hx.dtype) * jax.nn.silu(zx)) @ w_out
    out_c = ((yc.astype(hc.dtype) * jax.nn.silu(zc)) @ w_out) if with_ctx else None
    return out_c, out_x


def mlstm_proj(h, w_in, gate_b):
    bsz, t, _ = h.shape
    splits = [MLSTM_QK, 2 * MLSTM_QK, 2 * MLSTM_QK + MLSTM_WIDTH, 2 * MLSTM_QK + 2 * MLSTM_WIDTH]
    q, k, v, z, g = jnp.split(h @ w_in, splits, axis=-1)
    heads = lambda y, d: y.reshape(bsz, t, MLSTM_HEADS, d).transpose(0, 2, 1, 3).astype(jnp.float32)
    q = heads(q, MLSTM_DK)
    k = heads(k, MLSTM_DK) * (MLSTM_DK ** -0.5)
    v = heads(v, MLSTM_DV)
    g = g.reshape(bsz, t, 2, 2, MLSTM_HEADS).astype(jnp.float32) + gate_b.astype(jnp.float32)
    g = g.transpose(2, 3, 0, 4, 1)
    log_i = g[:, 0]
    log_f = jax.nn.log_sigmoid(g[:, 1])
    return q, k, v, z, log_i, log_f


def mlstm_chunk_scan(q, k, v, log_i, log_f, state):
    bsz, nh, t, _ = q.shape
    L = MLSTM_CHUNK
    nc = t // L
    chunks = lambda y: jnp.moveaxis(y.reshape(bsz, nh, nc, L, *y.shape[3:]), 2, 0)
    causal = jnp.tril(jnp.ones((L, L), dtype=bool))

    def step(carry, inp):
        C, n, m = carry
        qc, kc, vc, ic, fc = inp
        bcum = jnp.cumsum(fc, axis=-1)
        logw = jnp.where(causal, bcum[..., :, None] - bcum[..., None, :] + ic[..., None, :], -jnp.inf)
        g = bcum + m[..., None]
        m_t = jnp.maximum(g, jnp.max(logw, axis=-1))
        s = jnp.einsum('bhtd,bhsd->bhts', qc, kc) * jnp.exp(logw - m_t[..., None])
        inter = jnp.exp(g - m_t)
        num = jnp.einsum('bhts,bhsv->bhtv', s, vc) + inter[..., None] * jnp.einsum('bhtd,bhdv->bhtv', qc, C)
        den = jnp.sum(s, axis=-1) + inter * jnp.einsum('bhtd,bhd->bht', qc, n)
        h = num / jnp.maximum(jnp.abs(den), jnp.exp(-m_t))[..., None]
        m_new = m_t[..., -1]
        wgt = jnp.exp(bcum[..., -1:] - bcum + ic - m_new[..., None])
        decay = jnp.exp(bcum[..., -1] + m - m_new)
        C = decay[..., None, None] * C + jnp.einsum('bhs,bhsd,bhsv->bhdv', wgt, kc, vc)
        n = decay[..., None] * n + jnp.einsum('bhs,bhsd->bhd', wgt, kc)
        return (C, n, m_new), h

    state, hs = lax.scan(step, state, (chunks(q), chunks(k), chunks(v), chunks(log_i), chunks(log_f)))
    return jnp.moveaxis(hs, 0, 2).reshape(bsz, nh, t, -1), state


def mlstm_bidir(q, k, v, log_i, log_f, states):
    flip = lambda y: jnp.flip(y, axis=2)
    hf, sf = mlstm_chunk_scan(q, k, v, log_i[0], log_f[0], states[0])
    hb, sb = mlstm_chunk_scan(flip(q), flip(k), flip(v), flip(log_i[1]), flip(log_f[1]), states[1])
    return hf + flip(hb), (sf, sb)


def mlstm_zero_state(bsz):
    return (jnp.zeros((bsz, MLSTM_HEADS, MLSTM_DK, MLSTM_DV), jnp.float32),
            jnp.zeros((bsz, MLSTM_HEADS, MLSTM_DK), jnp.float32),
            jnp.zeros((bsz, MLSTM_HEADS), jnp.float32))


def mlstm_mixer(hc, hx, w_in, gate_b, norm_g, w_out, with_ctx):
    def finish(hsum, z):
        bsz, _, t, _ = hsum.shape
        hn = hsum * lax.rsqrt(jnp.mean(hsum * hsum, axis=-1, keepdims=True) + NORM_EPS)
        hn = hn.transpose(0, 2, 1, 3).reshape(bsz, t, MLSTM_WIDTH) * norm_g
        return (hn.astype(z.dtype) * jax.nn.silu(z)) @ w_out
    qc, kc, vc, zc, ic, fc = mlstm_proj(hc, w_in, gate_b)
    zero = mlstm_zero_state(hc.shape[0])
    hcs, states_c = mlstm_bidir(qc, kc, vc, ic, fc, (zero, zero))
    qx, kx, vx, zx, ix, fx = mlstm_proj(hx, w_in, gate_b)
    hxs, _ = mlstm_bidir(qx, kx, vx, ix, fx, states_c)
    return (finish(hcs, zc) if with_ctx else None), finish(hxs, zx)


def shift_grid(h, rows):
    bsz, t, d = h.shape
    q = d // 4
    g = h.reshape(bsz, rows, GRID_W, d)
    left = jnp.pad(g[:, :, :-1, :q], ((0, 0), (0, 0), (1, 0), (0, 0)))
    right = jnp.pad(g[:, :, 1:, q:2 * q], ((0, 0), (0, 0), (0, 1), (0, 0)))
    up = jnp.pad(g[:, :-1, :, 2 * q:3 * q], ((0, 0), (1, 0), (0, 0), (0, 0)))
    down = jnp.pad(g[:, 1:, :, 3 * q:], ((0, 0), (0, 1), (0, 0), (0, 0)))
    return jnp.concatenate([left, right, up, down], axis=-1).reshape(bsz, t, d)


def shift_seq(h):
    half = h.shape[-1] // 2
    prev = jnp.pad(h[:, :-1, :half], ((0, 0), (1, 0), (0, 0)))
    nxt = jnp.pad(h[:, 1:, half:], ((0, 0), (0, 1), (0, 0)))
    return jnp.concatenate([prev, nxt], axis=-1)


def rwkv7_prep(h, shifted, mu, w_rkvz, w0, w1, w2, a0, a1, a2, k_k, k_a):
    hd = lambda y: y.reshape(*y.shape[:-1], RWKV_HEADS, RWKV_HEAD).astype(jnp.float32)
    xs = h[None] + (shifted - h)[None] * mu[:, None, None, :]
    r, k, v, z = jnp.einsum('gbtd,gde->gbte', xs[:4], w_rkvz)
    w_pre = w0[:, None, None, :] + jnp.einsum('xbtr,xre->xbte', jnp.tanh(jnp.einsum('btd,xdr->xbtr', xs[4], w1)), w2)
    decay = jnp.exp(-jnp.exp(-jax.nn.softplus(-w_pre.astype(jnp.float32)) - 0.5))
    a = jax.nn.sigmoid(a0[:, None, None, :] + jnp.einsum('xbtr,xre->xbte', jnp.einsum('btd,xdr->xbtr', xs[5], a1), a2))
    kk = hd(k * k_k)
    kk = kk / jnp.maximum(jnp.sqrt(jnp.sum(kk * kk, axis=-1, keepdims=True)), 1e-12)
    k_dir = k[None] * (1.0 + (a - 1.0) * k_a)
    return hd(r), hd(decay), kk, hd(a), hd(k_dir), hd(v), z


def rwkv7_scan(r, w, kk, a, k, v, s0, reverse):
    tm = lambda y: jnp.moveaxis(y, 1, 0)
    def step(s, inp):
        r_t, w_t, kk_t, a_t, k_t, v_t = inp
        s = (s * w_t[..., None, :]
             - jnp.einsum('bhvk,bhk->bhv', s, kk_t)[..., None] * (kk_t * a_t)[..., None, :]
             + v_t[..., :, None] * k_t[..., None, :])
        return s, jnp.einsum('bhvk,bhk->bhv', s, r_t)
    s_last, ys = lax.scan(step, s0, tuple(tm(y) for y in (r, w, kk, a, k, v)), reverse=reverse)
    return jnp.moveaxis(ys, 0, 1), s_last


def rwkv7_mixer(hc, hx, rows, mu, w_rkvz, w0, w1, w2, a0, a1, a2, k_k, k_a, r_k, ln_g, ln_b, w_out, with_ctx):
    def run(h, shifted, states):
        r, w, kk, a, k, v, z = rwkv7_prep(h, shifted, mu, w_rkvz, w0, w1, w2, a0, a1, a2, k_k, k_a)
        yf, sf = rwkv7_scan(r, w[0], kk, a[0], k[0], v, states[0], False)
        yb, sb = rwkv7_scan(r, w[1], kk, a[1], k[1], v, states[1], True)
        return (yf + yb, r, k, v, z), jnp.stack([sf, sb])

    def finish(y, r, k, v, z):
        mean = jnp.mean(y, axis=-1, keepdims=True)
        var = jnp.mean(jnp.square(y - mean), axis=-1, keepdims=True)
        y = ((y - mean) * lax.rsqrt(var + GN_EPS) * ln_g.reshape(RWKV_HEADS, RWKV_HEAD)
             + ln_b.reshape(RWKV_HEADS, RWKV_HEAD))
        bonus = jnp.sum(r[None] * k * r_k.reshape(RWKV_HEADS, RWKV_HEAD), axis=(0, -1))[..., None] * v
        y = (y + bonus).reshape(z.shape)
        return (y.astype(z.dtype) * jax.nn.silu(z)) @ w_out

    s0 = jnp.zeros((2, hc.shape[0], RWKV_HEADS, RWKV_HEAD, RWKV_HEAD), jnp.float32)
    parts_c, states_c = run(hc, shift_seq(hc), s0)
    parts_x, _ = run(hx, shift_grid(hx, rows), states_c)
    return (finish(*parts_c) if with_ctx else None), finish(*parts_x)


def setup_inputs(seed: int = 0) -> dict:
    key = jax.random.key(seed)
    ks = iter(jax.random.split(key, 40))
    nrm = lambda shape, scale: scale * jax.random.normal(next(ks), shape, jnp.float32)
    unif = lambda shape, lo, hi: jax.random.uniform(next(ks), shape, jnp.float32, lo, hi)
    D = D_MODEL
    lam_p = unif((N_LRU, 2, LRU_WIDTH), 0.9, 0.999) ** (1.0 / LRU_C)
    mlstm_gate_b = jnp.stack([nrm((N_MLSTM, 2, MLSTM_HEADS), 0.1),
                              unif((N_MLSTM, 2, MLSTM_HEADS), 3.0, 6.0)], axis=2)
    return {
        'x': nrm((BATCH, SEQ, D), 1.0),
        'c': nrm((BATCH, D), 1.0),
        'ctx': nrm((BATCH, CTX_LEN, D), 1.0),
        'c_ctx': nrm((D,), 1.0),
        'norm_g': 1.0 + nrm((DEPTH, D), 0.1),
        'mod_w': nrm((DEPTH, D, 3 * D), D ** -0.5),
        'mod_b': nrm((DEPTH, 3 * D), 0.02),
        'final_g': 1.0 + nrm((D,), 0.1),
        'lru_w_in': nrm((N_LRU, D, 2 * LRU_WIDTH), D ** -0.5),
        'lru_conv_w': nrm((N_LRU, CONV_W, LRU_WIDTH), CONV_W ** -0.5),
        'lru_conv_b': nrm((N_LRU, LRU_WIDTH), 0.02),
        'lru_gate_w': nrm((N_LRU, 2, 2, LRU_BLOCKS, LRU_BS, LRU_BS), LRU_BS ** -0.5),
        'lru_gate_b': nrm((N_LRU, 2, 2, LRU_WIDTH), 0.1),
        'lru_lam': jnp.log(lam_p) - jnp.log1p(-lam_p),
        'lru_w_out': nrm((N_LRU, LRU_WIDTH, D), LRU_WIDTH ** -0.5),
        'mlstm_w_in': nrm((N_MLSTM, D, 2 * MLSTM_QK + 2 * MLSTM_WIDTH + 4 * MLSTM_HEADS), D ** -0.5),
        'mlstm_gate_b': mlstm_gate_b,
        'mlstm_norm_g': 1.0 + nrm((N_MLSTM, MLSTM_WIDTH), 0.1),
        'mlstm_w_out': nrm((N_MLSTM, MLSTM_WIDTH, D), MLSTM_WIDTH ** -0.5),
        'r7_mu': unif((N_RWKV, 6, D), 0.0, 1.0),
        'r7_w_rkvz': nrm((N_RWKV, 4, D, RWKV_WIDTH), D ** -0.5),
        'r7_w0': unif((N_RWKV, 2, RWKV_WIDTH), -6.0, -1.0),
        'r7_w1': nrm((N_RWKV, 2, D, DECAY_RANK), D ** -0.5),
        'r7_w2': nrm((N_RWKV, 2, DECAY_RANK, RWKV_WIDTH), 0.1 * DECAY_RANK ** -0.5),
        'r7_a0': nrm((N_RWKV, 2, RWKV_WIDTH), 0.1),
        'r7_a1': nrm((N_RWKV, 2, D, ICL_RANK), D ** -0.5),
        'r7_a2': nrm((N_RWKV, 2, ICL_RANK, RWKV_WIDTH), 0.1 * ICL_RANK ** -0.5),
        'r7_k_k': 0.85 + nrm((N_RWKV, RWKV_WIDTH), 0.1),
        'r7_k_a': 1.0 + nrm((N_RWKV, RWKV_WIDTH), 0.1),
        'r7_r_k': nrm((N_RWKV, RWKV_WIDTH), 0.1),
        'r7_ln_g': 1.0 + nrm((N_RWKV, RWKV_WIDTH), 0.1),
        'r7_ln_b': nrm((N_RWKV, RWKV_WIDTH), 0.02),
        'r7_w_out': nrm((N_RWKV, RWKV_WIDTH, D), RWKV_WIDTH ** -0.5),
    }


def reference(x, c, ctx, c_ctx, norm_g, mod_w, mod_b, final_g,
              lru_w_in, lru_conv_w, lru_conv_b, lru_gate_w, lru_gate_b, lru_lam, lru_w_out,
              mlstm_w_in, mlstm_gate_b, mlstm_norm_g, mlstm_w_out,
              r7_mu, r7_w_rkvz, r7_w0, r7_w1, r7_w2, r7_a0, r7_a1, r7_a2,
              r7_k_k, r7_k_a, r7_r_k, r7_ln_g, r7_ln_b, r7_w_out):
    rows = x.shape[1] // GRID_W
    xc = ctx
    for i in range(DEPTH):
        kind, j = i % N_MIXERS, i // N_MIXERS
        with_ctx = i < DEPTH - 1
        shift_x, scale_x, gate_x = ada_params(c, mod_w[i], mod_b[i])
        shift_c, scale_c, gate_c = ada_params(c_ctx, mod_w[i], mod_b[i])
        hx = rmsnorm(x, norm_g[i]) * (1.0 + scale_x[:, None]) + shift_x[:, None]
        hc = rmsnorm(xc, norm_g[i]) * (1.0 + scale_c) + shift_c
        if kind == 0:
            yc, yx = lru_mixer(hc, hx, lru_w_in[j], lru_conv_w[j], lru_conv_b[j], lru_gate_w[j],
                               lru_gate_b[j], lru_lam[j], lru_w_out[j], with_ctx)
        elif kind == 1:
            yc, yx = mlstm_mixer(hc, hx, mlstm_w_in[j], mlstm_gate_b[j], mlstm_norm_g[j], mlstm_w_out[j], with_ctx)
        else:
            yc, yx = rwkv7_mixer(hc, hx, rows, r7_mu[j], r7_w_rkvz[j], r7_w0[j], r7_w1[j], r7_w2[j],
                                 r7_a0[j], r7_a1[j], r7_a2[j], r7_k_k[j], r7_k_a[j], r7_r_k[j],
                                 r7_ln_g[j], r7_ln_b[j], r7_w_out[j], with_ctx)
        x = x + gate_x[:, None] * yx
        if with_ctx:
            xc = xc + gate_c * yc
    return rmsnorm(x, final_g)
```

```python
import functools

import jax
import jax.numpy as jnp
from jax import lax
from jax.experimental import pallas as pl
from jax.experimental.pallas import tpu as pltpu

F32 = jnp.float32
BF16 = jnp.bfloat16
HI = lax.Precision.HIGHEST

GRID_W = 64
LRU_C = 8.0
LRU_BS = 128
CONV_W = 4
MLSTM_HEADS = 8
MLSTM_DK = 128
MLSTM_DV = 256
MLSTM_CHUNK = 64
RWKV_HEAD = 64
RWKV_CHUNK = 16
GN_EPS = 64e-5
NORM_EPS = 1e-6

TM = 256
SUBLANES = 8
LANES = 128
VMEM_LIMIT = 56 * 1024 * 1024


def _cp(sem):
    return pltpu.CompilerParams(dimension_semantics=sem, vmem_limit_bytes=VMEM_LIMIT)


def _silu(v):
    return v * jax.nn.sigmoid(v)


def _softplus(v):
    return jnp.maximum(v, 0.0) + jnp.log1p(jnp.exp(-jnp.abs(v)))


def _log_sigmoid(v):
    return jnp.minimum(v, 0.0) - jnp.log1p(jnp.exp(-jnp.abs(v)))


def _bdot(a, b):
    return jnp.dot(a.astype(BF16), b.astype(BF16), preferred_element_type=F32)


def _hdot(a, b):
    return jnp.dot(a, b, precision=HI, preferred_element_type=F32)


def _dot_nt(a, b, precision=None):
    return lax.dot_general(a, b, (((1,), (1,)), ((), ())), precision=precision,
                           preferred_element_type=F32)


def _dot_tn(a, b, precision=None):
    return lax.dot_general(a, b, (((0,), (0,)), ((), ())), precision=precision,
                           preferred_element_type=F32)


def _normmod(xv, g, mod_ref):
    y = xv * lax.rsqrt(jnp.mean(xv * xv, axis=-1, keepdims=True) + NORM_EPS)
    return (y * g) * (1.0 + mod_ref[1:2, :]) + mod_ref[0:1, :]


def _tile_of_step(s, nt, nct, reverse):
    if not reverse:
        return s
    return jnp.where(s < nct, nct - 1 - s, nt - 1 - (s - nct))


def _mod_index(nct):
    return lambda b, i: (b * 2 + jnp.where(i >= nct, 1, 0), 0, 0)


def _mod_kernel(c_ref, w_ref, b_ref, o_ref):
    o_ref[...] = _hdot(_silu(c_ref[...]), w_ref[...]) + b_ref[...]


def _mod_params(cc, mod_w, mod_b):
    depth, d, d3 = mod_w.shape
    rows = cc.shape[0]
    return pl.pallas_call(
        _mod_kernel,
        grid=(depth, d3 // d),
        in_specs=[pl.BlockSpec((rows, d), lambda l, j: (0, 0)),
                  pl.BlockSpec((None, d, d), lambda l, j: (l, 0, j)),
                  pl.BlockSpec((None, 1, d), lambda l, j: (l, 0, j))],
        out_specs=pl.BlockSpec((None, rows, d), lambda l, j: (l, 0, j)),
        out_shape=jax.ShapeDtypeStruct((depth, rows, d3), F32),
        compiler_params=_cp(("arbitrary", "arbitrary")),
    )(cc, mod_w, mod_b.reshape(depth, 1, d3))


def _nm_matmul_kernel(x_ref, mod_ref, g_ref, w_ref, b_ref, o_ref):
    h = _normmod(x_ref[...], g_ref[...], mod_ref)
    o_ref[...] = _bdot(h, w_ref[...]) + b_ref[...]


def _nm_matmul(xs, mods, g, w, bias, tn, nct):
    bsz, tt, d = xs.shape
    n = w.shape[1]
    return pl.pallas_call(
        _nm_matmul_kernel,
        grid=(n // tn, bsz, tt // TM),
        in_specs=[pl.BlockSpec((None, TM, d), lambda j, b, i: (b, i, 0)),
                  pl.BlockSpec((None, 3, d), lambda j, b, i: (b * 2 + jnp.where(i >= nct, 1, 0), 0, 0)),
                  pl.BlockSpec((1, d), lambda j, b, i: (0, 0)),
                  pl.BlockSpec((d, tn), lambda j, b, i: (0, j)),
                  pl.BlockSpec((1, tn), lambda j, b, i: (0, j))],
        out_specs=pl.BlockSpec((None, TM, tn), lambda j, b, i: (b, i, j)),
        out_shape=jax.ShapeDtypeStruct((bsz, tt, n), F32),
        compiler_params=_cp(("arbitrary", "arbitrary", "arbitrary")),
    )(xs, mods, g, w, bias)


def _lru_out_kernel(x_ref, mod_ref, y0_ref, y1_ref, z_ref, w_ref, o_ref):
    y = (y0_ref[...] + y1_ref[...]) * _silu(z_ref[...])
    o_ref[...] = x_ref[...] + mod_ref[2:3, :] * _bdot(y, w_ref[...])


def _mlstm_out_kernel(x_ref, mod_ref, h0_ref, h1_ref, z_ref, ng_ref, w_ref, o_ref, hn_s):
    hs = h0_ref[...] + h1_ref[...]
    for hd in range(MLSTM_HEADS):
        sl = slice(hd * MLSTM_DV, (hd + 1) * MLSTM_DV)
        hh = hs[:, sl]
        hn_s[:, sl] = hh * lax.rsqrt(jnp.mean(hh * hh, axis=-1, keepdims=True) + NORM_EPS)
    y = (hn_s[...] * ng_ref[...]) * _silu(z_ref[...])
    o_ref[...] = x_ref[...] + mod_ref[2:3, :] * _bdot(y, w_ref[...])


def _rwkv_out_kernel(x_ref, mod_ref, y0_ref, y1_ref, r_ref, k0_ref, k1_ref, v_ref, z_ref,
                     rk_ref, lg_ref, lb_ref, hm_ref, w_ref, o_ref):
    hm = hm_ref[...]
    y = y0_ref[...] + y1_ref[...]
    mean = _hdot(y, hm)
    yc = y - mean
    var = _hdot(yc * yc, hm)
    yn = yc * lax.rsqrt(var + GN_EPS) * lg_ref[...] + lb_ref[...]
    rv = r_ref[...]
    bonus = _hdot(rv * k0_ref[...] * rk_ref[...] + rv * k1_ref[...] * rk_ref[...], hm) * float(RWKV_HEAD)
    out = (yn + bonus * v_ref[...]) * _silu(z_ref[...])
    o_ref[...] = x_ref[...] + mod_ref[2:3, :] * _bdot(out, w_ref[...])


def _lru_scan_kernel(u_ref, up_ref, un_ref, cw_ref, cb_ref, gw_ref, gb_ref, lam_ref, y_ref,
                     ext, a_s, b_s, h_s, *, nt, nct, reverse):
    s = pl.program_id(1)
    tile = _tile_of_step(s, nt, nct, reverse)
    seg_first = jnp.logical_or(tile == 0, tile == nct)
    seg_last = jnp.logical_or(tile == nct - 1, tile == nt - 1)
    halo = SUBLANES

    @pl.when(s == 0)
    def _():
        h_s[...] = jnp.zeros_like(h_s)

    ext[0:halo, :] = jnp.where(seg_first, 0.0, up_ref[...])
    ext[halo:halo + TM, :] = u_ref[...]
    ext[halo + TM:2 * halo + TM, :] = jnp.where(seg_last, 0.0, un_ref[...])
    left = CONV_W // 2
    cv = cb_ref[...]
    for j in range(CONV_W):
        cv = cv + ext[pl.ds(halo - left + j, TM), :] * cw_ref[j:j + 1, :]

    nblk = cv.shape[1] // LRU_BS
    neg_c_sp = -LRU_C * _softplus(-lam_ref[...])
    for n in range(nblk):
        sl = slice(n * LRU_BS, (n + 1) * LRU_BS)
        blk = cv[:, sl]
        gates = _bdot(blk, gw_ref[n])
        r = jax.nn.sigmoid(gates[:, :LRU_BS] + gb_ref[0:1, sl])
        i = jax.nn.sigmoid(gates[:, LRU_BS:] + gb_ref[1:2, sl])
        log_a = neg_c_sp[:, sl] * r
        a = jnp.exp(log_a)
        a_s[:, sl] = a
        b_s[:, sl] = jnp.sqrt(-jnp.tanh(log_a) * (a * a + 1.0)) * (i * blk)

    def body(j, h):
        t = TM - 1 - j if reverse else j
        h = a_s[pl.ds(t, 1), :] * h + b_s[pl.ds(t, 1), :]
        y_ref[pl.ds(t, 1), :] = h
        return h

    h_s[...] = lax.fori_loop(0, TM, body, h_s[...], unroll=8)


def _lru_scan(uz, conv_w, conv_b, gw, gb, lam, nct, reverse):
    bsz, tt, e2 = uz.shape
    e = e2 // 2
    nt = tt // TM
    r8 = TM // SUBLANES
    nb8 = tt // SUBLANES
    tile = lambda s: _tile_of_step(s, nt, nct, reverse)
    kern = functools.partial(_lru_scan_kernel, nt=nt, nct=nct, reverse=reverse)
    return pl.pallas_call(
        kern,
        grid=(bsz, nt),
        in_specs=[pl.BlockSpec((None, TM, e), lambda b, s: (b, tile(s), 0)),
                  pl.BlockSpec((None, SUBLANES, e), lambda b, s: (b, jnp.maximum(tile(s) * r8 - 1, 0), 0)),
                  pl.BlockSpec((None, SUBLANES, e), lambda b, s: (b, jnp.minimum((tile(s) + 1) * r8, nb8 - 1), 0)),
                  pl.BlockSpec((CONV_W, e), lambda b, s: (0, 0)),
                  pl.BlockSpec((1, e), lambda b, s: (0, 0)),
                  pl.BlockSpec(gw.shape, lambda b, s: (0, 0, 0)),
                  pl.BlockSpec((2, e), lambda b, s: (0, 0)),
                  pl.BlockSpec((1, e), lambda b, s: (0, 0))],
        out_specs=pl.BlockSpec((None, TM, e), lambda b, s: (b, tile(s), 0)),
        out_shape=jax.ShapeDtypeStruct((bsz, tt, e), F32),
        scratch_shapes=[pltpu.VMEM((TM + 2 * SUBLANES, e), F32),
                        pltpu.VMEM((TM, e), F32),
                        pltpu.VMEM((TM, e), F32),
                        pltpu.VMEM((1, e), F32)],
        compiler_params=_cp(("arbitrary", "arbitrary")),
    )(uz, uz, uz, conv_w, conv_b, gw, gb, lam)


def _lru_layer(xs, mods, norm_g, w_in, conv_w, conv_b, gate_w, gate_b, lam, w_out, nct):
    bsz, tt, d = xs.shape
    e = w_out.shape[0]
    nblk = e // LRU_BS
    uz = _nm_matmul(xs, mods, norm_g.reshape(1, d), w_in.astype(BF16),
                    jnp.zeros((1, 2 * e), F32), e, nct)
    ys = []
    for dr in range(2):
        gw = jnp.transpose(gate_w[dr], (1, 2, 0, 3)).reshape(nblk, LRU_BS, 2 * LRU_BS).astype(BF16)
        ys.append(_lru_scan(uz, conv_w, conv_b.reshape(1, e), gw, gate_b[dr], lam[dr].reshape(1, e),
                            nct, reverse=(dr == 1)))
    return pl.pallas_call(
        _lru_out_kernel,
        grid=(bsz, tt // TM),
        in_specs=[pl.BlockSpec((None, TM, d), lambda b, i: (b, i, 0)),
                  pl.BlockSpec((None, 3, d), _mod_index(nct)),
                  pl.BlockSpec((None, TM, e), lambda b, i: (b, i, 0)),
                  pl.BlockSpec((None, TM, e), lambda b, i: (b, i, 0)),
                  pl.BlockSpec((None, TM, e), lambda b, i: (b, i, 1)),
                  pl.BlockSpec((e, d), lambda b, i: (0, 0))],
        out_specs=pl.BlockSpec((None, TM, d), lambda b, i: (b, i, 0)),
        out_shape=jax.ShapeDtypeStruct(xs.shape, F32),
        compiler_params=_cp(("arbitrary", "arbitrary")),
    )(xs, mods, ys[0], ys[1], uz, w_out.astype(BF16))


def _mlstm_scan_kernel(q_ref, k_ref, v_ref, g_ref, it_ref, ft_ref, o_ref, c_s, n_s, m_s,
                       *, dr, reverse):
    hd = pl.program_id(1)
    s = pl.program_id(2)
    L = MLSTM_CHUNK
    nchunk = TM // L

    @pl.when(s == 0)
    def _():
        c_s[...] = jnp.zeros_like(c_s)
        n_s[...] = jnp.zeros_like(n_s)
        m_s[...] = jnp.zeros_like(m_s)

    row = lax.broadcasted_iota(jnp.int32, (L, L), 0)
    col = lax.broadcasted_iota(jnp.int32, (L, L), 1)
    causal = (col >= row) if reverse else (col <= row)
    causal_t = (row >= col) if reverse else (row <= col)
    lane = lax.broadcasted_iota(jnp.int32, (L, LANES), 1)
    i_lane = dr * 2 * MLSTM_HEADS + hd
    f_lane = i_lane + MLSTM_HEADS
    last = 0 if reverse else L - 1

    def chunk(j, carry):
        c = nchunk - 1 - j if reverse else j
        r0 = pl.multiple_of(c * L, L)
        q = q_ref[pl.ds(r0, L), :]
        k = k_ref[pl.ds(r0, L), :] * (MLSTM_DK ** -0.5)
        v = v_ref[pl.ds(r0, L), :]
        g = g_ref[pl.ds(r0, L), :]
        i_col = jnp.sum(jnp.where(lane == i_lane, g, 0.0), axis=1, keepdims=True)
        f_col = _log_sigmoid(jnp.sum(jnp.where(lane == f_lane, g, 0.0), axis=1, keepdims=True))
        i_row = it_ref[pl.ds(c, 1), :]
        f_row = _log_sigmoid(ft_ref[pl.ds(c, 1), :])
        bcum_col = jnp.sum(jnp.where(causal, f_row, 0.0), axis=1, keepdims=True)
        bcum_row = jnp.sum(jnp.where(causal_t, f_col, 0.0), axis=0, keepdims=True)
        logw = jnp.where(causal, bcum_col - bcum_row + i_row, -jnp.inf)
        m_prev = m_s[...]
        gq = bcum_col + m_prev
        m_t = jnp.maximum(gq, jnp.max(logw, axis=1, keepdims=True))
        sm = _dot_nt(q.astype(BF16), k.astype(BF16)) * jnp.exp(logw - m_t)
        inter = jnp.exp(gq - m_t)
        num = _bdot(sm, v) + inter * _bdot(q, c_s[...])
        den = jnp.sum(sm, axis=1, keepdims=True) + inter * jnp.sum(q * n_s[...], axis=1, keepdims=True)
        o_ref[pl.ds(r0, L), :] = num / jnp.maximum(jnp.abs(den), jnp.exp(-m_t))
        m_new = m_t[last:last + 1, :]
        b_last = bcum_col[last:last + 1, :]
        wgt = jnp.exp(b_last - bcum_col + i_col - m_new)
        decay = jnp.exp(b_last + m_prev - m_new)
        kw = k * wgt
        c_s[...] = decay * c_s[...] + _dot_tn(kw.astype(BF16), v.astype(BF16))
        n_s[...] = decay * n_s[...] + jnp.sum(kw, axis=0, keepdims=True)
        m_s[...] = m_new
        return carry

    lax.fori_loop(0, nchunk, chunk, 0)


def _mlstm_scan(qkvz, g, gt, nct, dr):
    bsz, tt, _ = qkvz.shape
    nt = tt // TM
    reverse = dr == 1
    nh = MLSTM_HEADS
    nchunk = TM // MLSTM_CHUNK
    tile = lambda s: _tile_of_step(s, nt, nct, reverse)
    kern = functools.partial(_mlstm_scan_kernel, dr=dr, reverse=reverse)
    koff = nh
    voff = 2 * nh * MLSTM_DK // MLSTM_DV
    return pl.pallas_call(
        kern,
        grid=(bsz, nh, nt),
        in_specs=[pl.BlockSpec((None, TM, MLSTM_DK), lambda b, h, s: (b, tile(s), h)),
                  pl.BlockSpec((None, TM, MLSTM_DK), lambda b, h, s: (b, tile(s), koff + h)),
                  pl.BlockSpec((None, TM, MLSTM_DV), lambda b, h, s: (b, tile(s), voff + h)),
                  pl.BlockSpec((None, TM, LANES), lambda b, h, s: (b, tile(s), 0)),
                  pl.BlockSpec((None, None, None, nchunk, MLSTM_CHUNK),
                               lambda b, h, s: (b, dr * 2 * nh + h, tile(s), 0, 0)),
                  pl.BlockSpec((None, None, None, nchunk, MLSTM_CHUNK),
                               lambda b, h, s: (b, dr * 2 * nh + nh + h, tile(s), 0, 0))],
        out_specs=pl.BlockSpec((None, TM, MLSTM_DV), lambda b, h, s: (b, tile(s), h)),
        out_shape=jax.ShapeDtypeStruct((bsz, tt, nh * MLSTM_DV), F32),
        scratch_shapes=[pltpu.VMEM((MLSTM_DK, MLSTM_DV), F32),
                        pltpu.VMEM((1, MLSTM_DK), F32),
                        pltpu.VMEM((1, 1), F32)],
        compiler_params=_cp(("arbitrary", "arbitrary", "arbitrary")),
    )(qkvz, qkvz, qkvz, g, gt, gt)


def _mlstm_layer(xs, mods, norm_g, w_in, gate_b, hnorm_g, w_out, nct):
    bsz, tt, d = xs.shape
    nh = MLSTM_HEADS
    qk = nh * MLSTM_DK
    width = nh * MLSTM_DV
    nmain = 2 * qk + 2 * width
    ng = 4 * nh
    g2 = norm_g.reshape(1, d)
    qkvz = _nm_matmul(xs, mods, g2, w_in[:, :nmain].astype(BF16), jnp.zeros((1, nmain), F32),
                      width, nct)
    wg = jnp.pad(w_in[:, nmain:], ((0, 0), (0, LANES - ng))).astype(BF16)
    bg = jnp.pad(gate_b.reshape(1, ng), ((0, 0), (0, LANES - ng)))
    g = _nm_matmul(xs, mods, g2, wg, bg, LANES, nct)
    gt = jnp.transpose(g[:, :, :ng], (0, 2, 1)).reshape(bsz, ng, tt // TM, TM // MLSTM_CHUNK, MLSTM_CHUNK)
    hs = [_mlstm_scan(qkvz, g, gt, nct, dr) for dr in range(2)]
    return pl.pallas_call(
        _mlstm_out_kernel,
        grid=(bsz, tt // TM),
        in_specs=[pl.BlockSpec((None, TM, d), lambda b, i: (b, i, 0)),
                  pl.BlockSpec((None, 3, d), _mod_index(nct)),
                  pl.BlockSpec((None, TM, width), lambda b, i: (b, i, 0)),
                  pl.BlockSpec((None, TM, width), lambda b, i: (b, i, 0)),
                  pl.BlockSpec((None, TM, width), lambda b, i: (b, i, (2 * qk + width) // width)),
                  pl.BlockSpec((1, width), lambda b, i: (0, 0)),
                  pl.BlockSpec((width, d), lambda b, i: (0, 0))],
        out_specs=pl.BlockSpec((None, TM, d), lambda b, i: (b, i, 0)),
        out_shape=jax.ShapeDtypeStruct(xs.shape, F32),
        scratch_shapes=[pltpu.VMEM((TM, width), F32)],
        compiler_params=_cp(("arbitrary", "arbitrary")),
    )(xs, mods, hs[0], hs[1], qkvz, hnorm_g.reshape(1, width), w_out.astype(BF16))


def _rwkv_prep_kernel(x_ref, xp_ref, xn_ref, mod_ref, g_ref, mu_ref, wr_ref, w0_ref, w1_ref, w2_ref,
                      a0_ref, a1_ref, a2_ref, kk_ref, ka_ref, hm_ref,
                      r_o, kn_o, v_o, z_o, lw0_o, lw1_o, be0_o, be1_o, k0_o, k1_o,
                      hs, sh, *, nt, nct):
    i = pl.program_id(1)
    seg_first = jnp.logical_or(i == 0, i == nct)
    seg_last = jnp.logical_or(i == nct - 1, i == nt - 1)
    gw = GRID_W
    d = x_ref.shape[1]
    g = g_ref[...]
    hs[0:gw, :] = jnp.where(seg_first, 0.0, _normmod(xp_ref[...], g, mod_ref))
    hs[gw:gw + TM, :] = _normmod(x_ref[...], g, mod_ref)
    hs[gw + TM:2 * gw + TM, :] = jnp.where(seg_last, 0.0, _normmod(xn_ref[...], g, mod_ref))

    @pl.when(i < nct)
    def _():
        half = d // 2
        sh[:, 0:half] = hs[pl.ds(gw - 1, TM), 0:half]
        sh[:, half:d] = hs[pl.ds(gw + 1, TM), half:d]

    @pl.when(i >= nct)
    def _():
        q = d // 4
        wcol = lax.broadcasted_iota(jnp.int32, (TM, q), 0) % gw
        sh[:, 0:q] = jnp.where(wcol == 0, 0.0, hs[pl.ds(gw - 1, TM), 0:q])
        sh[:, q:2 * q] = jnp.where(wcol == gw - 1, 0.0, hs[pl.ds(gw + 1, TM), q:2 * q])
        sh[:, 2 * q:3 * q] = hs[pl.ds(0, TM), 2 * q:3 * q]
        sh[:, 3 * q:d] = hs[pl.ds(2 * gw, TM), 3 * q:d]

    h = hs[pl.ds(gw, TM), :]
    dx = sh[...] - h
    mix = lambda n: h + dx * mu_ref[n:n + 1, :]
    hm = hm_ref[...]
    r = _bdot(mix(0), wr_ref[0])
    k = _bdot(mix(1), wr_ref[1])
    r_o[...] = r
    v_o[...] = _bdot(mix(2), wr_ref[2])
    z_o[...] = _bdot(mix(3), wr_ref[3])
    kk = k * kk_ref[...]
    ss = _hdot(kk * kk, hm) * float(RWKV_HEAD)
    kn = kk / jnp.maximum(jnp.sqrt(ss), 1e-12)
    kn_o[...] = kn
    xw = mix(4)
    xa = mix(5)
    for dr, (lw_o, be_o, k_o) in enumerate(((lw0_o, be0_o, k0_o), (lw1_o, be1_o, k1_o))):
        w_pre = w0_ref[dr:dr + 1, :] + _bdot(jnp.tanh(_bdot(xw, w1_ref[dr])), w2_ref[dr])
        lw_o[...] = -jnp.exp(-_softplus(-w_pre) - 0.5)
        a = jax.nn.sigmoid(a0_ref[dr:dr + 1, :] + _bdot(_bdot(xa, a1_ref[dr]), a2_ref[dr]))
        be_o[...] = a * kn
        k_o[...] = k * (1.0 + (a - 1.0) * ka_ref[...])


def _rwkv_scan_kernel(r_ref, kn_ref, v_ref, lw_ref, be_ref, kd_ref, y_ref, st_s, *, reverse):
    s = pl.program_id(2)
    L = RWKV_CHUNK
    N = RWKV_HEAD
    nchunk = TM // L
    nhead = r_ref.shape[1] // N

    @pl.when(s == 0)
    def _():
        st_s[...] = jnp.zeros_like(st_s)

    row = lax.broadcasted_iota(jnp.int32, (L, L), 0)
    col = lax.broadcasted_iota(jnp.int32, (L, L), 1)
    incl = (col >= row) if reverse else (col <= row)
    strict = (col > row) if reverse else (col < row)
    tri = incl.astype(F32)
    eye = (col == row).astype(F32)
    last = 0 if reverse else L - 1

    def chunk(j, carry):
        c = nchunk - 1 - j if reverse else j
        r0 = pl.multiple_of(c * L, L)
        rows = pl.ds(r0, L)
        lw = lw_ref[rows, :]
        lc = _hdot(tri, lw)
        lc_last = lc[last:last + 1, :]
        einv = jnp.exp(-lc)
        r_p = r_ref[rows, :] * jnp.exp(lc)
        a_p = kn_ref[rows, :] * jnp.exp(lc - lw)
        be = be_ref[rows, :]
        kd = kd_ref[rows, :]
        b_i = be * einv
        k_i = kd * einv
        to_end = jnp.exp(lc_last - lc)
        b_l = be * to_end
        k_l = kd * to_end
        p_l = jnp.exp(lc_last)
        v = v_ref[rows, :]
        for hh in range(nhead):
            sl = slice(hh * N, (hh + 1) * N)
            lhs = jnp.concatenate([a_p[:, sl], r_p[:, sl]], axis=0)
            rhs = jnp.concatenate([b_i[:, sl], k_i[:, sl]], axis=0)
            am = _dot_nt(lhs, rhs, HI)
            a_ab = jnp.where(strict, am[:L, :L], 0.0)
            a_ak = jnp.where(strict, am[:L, L:], 0.0)
            a_rb = jnp.where(incl, am[L:, :L], 0.0)
            a_rk = jnp.where(incl, am[L:, L:], 0.0)
            tinv = eye - a_ab
            pw = a_ab
            for _ in range(L.bit_length() - 2):
                pw = _hdot(pw, pw)
                tinv = _hdot(tinv, eye + pw)
            st = st_s[hh]
            gs = _dot_nt(lhs, st, HI)
            vh = v[:, sl]
            u = _hdot(tinv, gs[:L] + _hdot(a_ak, vh))
            y_ref[rows, sl] = gs[L:] - _hdot(a_rb, u) + _hdot(a_rk, vh)
            upd = _dot_tn(jnp.concatenate([-u, vh], axis=0),
                          jnp.concatenate([b_l[:, sl], k_l[:, sl]], axis=0), HI)
            st_s[hh] = st * p_l[:, sl] + upd
        return carry

    lax.fori_loop(0, nchunk, chunk, 0)


def _rwkv_scan(r, kn, v, lw, be, kd, nct, reverse):
    bsz, tt, w = r.shape
    nt = tt // TM
    tile = lambda s: _tile_of_step(s, nt, nct, reverse)
    spec = pl.BlockSpec((None, TM, LANES), lambda b, p, s: (b, tile(s), p))
    kern = functools.partial(_rwkv_scan_kernel, reverse=reverse)
    return pl.pallas_call(
        kern,
        grid=(bsz, w // LANES, nt),
        in_specs=[spec] * 6,
        out_specs=spec,
        out_shape=jax.ShapeDtypeStruct((bsz, tt, w), F32),
        scratch_shapes=[pltpu.VMEM((LANES // RWKV_HEAD, RWKV_HEAD, RWKV_HEAD), F32)],
        compiler_params=_cp(("arbitrary", "arbitrary", "arbitrary")),
    )(r, kn, v, lw, be, kd)


def _rwkv_layer(xs, mods, norm_g, mu, w_rkvz, w0, w1, w2, a0, a1, a2, k_k, k_a, r_k, ln_g, ln_b,
                w_out, nct):
    bsz, tt, d = xs.shape
    nt = tt // TM
    r64 = TM // GRID_W
    nb64 = tt // GRID_W
    hidx = jnp.arange(d) // RWKV_HEAD
    hmean = (hidx[:, None] == hidx[None, :]).astype(F32) / RWKV_HEAD
    row = lambda a: a.reshape(1, d)
    full = lambda a: pl.BlockSpec(a.shape, lambda b, i: (0,) * a.ndim)
    tile_spec = pl.BlockSpec((None, TM, d), lambda b, i: (b, i, 0))
    consts = [row(norm_g), mu, w_rkvz.astype(BF16), w0, w1.astype(BF16), w2.astype(BF16),
              a0, a1.astype(BF16), a2.astype(BF16), row(k_k), row(k_a), hmean]
    kern = functools.partial(_rwkv_prep_kernel, nt=nt, nct=nct)
    outs = pl.pallas_call(
        kern,
        grid=(bsz, nt),
        in_specs=[tile_spec,
                  pl.BlockSpec((None, GRID_W, d), lambda b, i: (b, jnp.maximum(i * r64 - 1, 0), 0)),
                  pl.BlockSpec((None, GRID_W, d), lambda b, i: (b, jnp.minimum((i + 1) * r64, nb64 - 1), 0)),
                  pl.BlockSpec((None, 3, d), _mod_index(nct))] + [full(a) for a in consts],
        out_specs=[tile_spec] * 10,
        out_shape=[jax.ShapeDtypeStruct(xs.shape, F32)] * 10,
        scratch_shapes=[pltpu.VMEM((TM + 2 * GRID_W, d), F32), pltpu.VMEM((TM, d), F32)],
        compiler_params=_cp(("arbitrary", "arbitrary")),
    )(xs, xs, xs, mods, *consts)
    r, kn, v, z, lw0, lw1, be0, be1, k0, k1 = outs
    y0 = _rwkv_scan(r, kn, v, lw0, be0, k0, nct, reverse=False)
    y1 = _rwkv_scan(r, kn, v, lw1, be1, k1, nct, reverse=True)
    fin = [row(r_k), row(ln_g), row(ln_b), hmean, w_out.astype(BF16)]
    return pl.pallas_call(
        _rwkv_out_kernel,
        grid=(bsz, nt),
        in_specs=[tile_spec, pl.BlockSpec((None, 3, d), _mod_index(nct))] + [tile_spec] * 7
                 + [full(a) for a in fin],
        out_specs=tile_spec,
        out_shape=jax.ShapeDtypeStruct(xs.shape, F32),
        compiler_params=_cp(("arbitrary", "arbitrary")),
    )(xs, mods, y0, y1, r, k0, k1, v, z, *fin)


def _final_norm_kernel(x_ref, g_ref, o_ref):
    xv = x_ref[...]
    o_ref[...] = xv * lax.rsqrt(jnp.mean(xv * xv, axis=-1, keepdims=True) + NORM_EPS) * g_ref[...]


def _final_norm(xs, g, nct):
    bsz, tt, d = xs.shape
    t = tt - nct * TM
    return pl.pallas_call(
        _final_norm_kernel,
        grid=(bsz, t // TM),
        in_specs=[pl.BlockSpec((None, TM, d), lambda b, i: (b, i + nct, 0)),
                  pl.BlockSpec((1, d), lambda b, i: (0, 0))],
        out_specs=pl.BlockSpec((None, TM, d), lambda b, i: (b, i, 0)),
        out_shape=jax.ShapeDtypeStruct((bsz, t, d), F32),
        compiler_params=_cp(("arbitrary", "arbitrary")),
    )(xs, g.reshape(1, d))


def kernel(x, c, ctx, c_ctx, norm_g, mod_w, mod_b, final_g, lru_w_in, lru_conv_w, lru_conv_b, lru_gate_w, lru_gate_b, lru_lam, lru_w_out, mlstm_w_in, mlstm_gate_b, mlstm_norm_g, mlstm_w_out, r7_mu, r7_w_rkvz, r7_w0, r7_w1, r7_w2, r7_a0, r7_a1, r7_a2, r7_k_k, r7_k_a, r7_r_k, r7_ln_g, r7_ln_b, r7_w_out):
    bsz, t, d = x.shape
    ctx_len = ctx.shape[1]
    depth = norm_g.shape[0]
    assert ctx_len % TM == 0 and t % TM == 0 and t % GRID_W == 0 and bsz < SUBLANES
    nct = ctx_len // TM

    cc = jnp.zeros((SUBLANES, d), F32).at[:bsz].set(c).at[bsz].set(c_ctx)
    mp = _mod_params(cc, mod_w, mod_b).reshape(depth, SUBLANES, 3, d)
    mods_all = jnp.stack([jnp.broadcast_to(mp[:, bsz][:, None], (depth, bsz, 3, d)), mp[:, :bsz]],
                         axis=2).reshape(depth, 2 * bsz, 3, d)

    xs = jnp.concatenate([ctx, x], axis=1)
    for i in range(depth):
        kind, j = i % 3, i // 3
        mods = mods_all[i]
        if kind == 0:
            xs = _lru_layer(xs, mods, norm_g[i], lru_w_in[j], lru_conv_w[j], lru_conv_b[j],
                            lru_gate_w[j], lru_gate_b[j], lru_lam[j], lru_w_out[j], nct)
        elif kind == 1:
            xs = _mlstm_layer(xs, mods, norm_g[i], mlstm_w_in[j], mlstm_gate_b[j], mlstm_norm_g[j],
                              mlstm_w_out[j], nct)
        else:
            xs = _rwkv_layer(xs, mods, norm_g[i], r7_mu[j], r7_w_rkvz[j], r7_w0[j], r7_w1[j], r7_w2[j],
                             r7_a0[j], r7_a1[j], r7_a2[j], r7_k_k[j], r7_k_a[j], r7_r_k[j],
                             r7_ln_g[j], r7_ln_b[j], r7_w_out[j], nct)
    return _final_norm(xs, final_g, nct)
```

```python
import functools

import jax
import jax.numpy as jnp
from jax import lax
from jax.experimental import pallas as pl
from jax.experimental.pallas import tpu as pltpu

F32 = jnp.float32
BF16 = jnp.bfloat16
HI = lax.Precision.HIGHEST

GRID_W = 64
LRU_C = 8.0
LRU_BS = 128
CONV_W = 4
MLSTM_HEADS = 8
MLSTM_DK = 128
MLSTM_DV = 256
MLSTM_CHUNK = 64
RWKV_HEAD = 64
RWKV_CHUNK = 64
GN_EPS = 64e-5
NORM_EPS = 1e-6

TM = 256
SUBLANES = 8
LANES = 128
VMEM_LIMIT = 56 * 1024 * 1024


def _cp(sem):
    return pltpu.CompilerParams(dimension_semantics=sem, vmem_limit_bytes=VMEM_LIMIT)


def _silu(v):
    return v * jax.nn.sigmoid(v)


def _softplus(v):
    return jnp.maximum(v, 0.0) + jnp.log1p(jnp.exp(-jnp.abs(v)))


def _log_sigmoid(v):
    return jnp.minimum(v, 0.0) - jnp.log1p(jnp.exp(-jnp.abs(v)))


def _bdot(a, b):
    return jnp.dot(a.astype(BF16), b.astype(BF16), preferred_element_type=F32)


def _hdot(a, b):
    return jnp.dot(a, b, precision=HI, preferred_element_type=F32)


def _dot_nt(a, b, precision=None):
    return lax.dot_general(a, b, (((1,), (1,)), ((), ())), precision=precision,
                           preferred_element_type=F32)


def _dot_tn(a, b, precision=None):
    return lax.dot_general(a, b, (((0,), (0,)), ((), ())), precision=precision,
                           preferred_element_type=F32)


def _normmod(xv, g, mod_ref):
    y = xv * lax.rsqrt(jnp.mean(xv * xv, axis=-1, keepdims=True) + NORM_EPS)
    return (y * g) * (1.0 + mod_ref[1:2, :]) + mod_ref[0:1, :]


def _tile_of_step(s, nt, nct, reverse):
    if not reverse:
        return s
    return jnp.where(s < nct, nct - 1 - s, nt - 1 - (s - nct))


def _mod_index(nct):
    return lambda b, i: (b * 2 + jnp.where(i >= nct, 1, 0), 0, 0)


def _mod_kernel(c_ref, w_ref, b_ref, o_ref):
    o_ref[...] = _hdot(_silu(c_ref[...]), w_ref[...]) + b_ref[...]


def _mod_params(cc, mod_w, mod_b):
    depth, d, d3 = mod_w.shape
    rows = cc.shape[0]
    return pl.pallas_call(
        _mod_kernel,
        grid=(depth, d3 // d),
        in_specs=[pl.BlockSpec((rows, d), lambda l, j: (0, 0)),
                  pl.BlockSpec((None, d, d), lambda l, j: (l, 0, j)),
                  pl.BlockSpec((None, 1, d), lambda l, j: (l, 0, j))],
        out_specs=pl.BlockSpec((None, rows, d), lambda l, j: (l, 0, j)),
        out_shape=jax.ShapeDtypeStruct((depth, rows, d3), F32),
        compiler_params=_cp(("arbitrary", "arbitrary")),
    )(cc, mod_w, mod_b.reshape(depth, 1, d3))


def _nm_matmul_kernel(x_ref, mod_ref, g_ref, w_ref, b_ref, o_ref):
    h = _normmod(x_ref[...], g_ref[...], mod_ref)
    o_ref[...] = _bdot(h, w_ref[...]) + b_ref[...]


def _nm_matmul(xs, mods, g, w, bias, tn, nct):
    bsz, tt, d = xs.shape
    n = w.shape[1]
    return pl.pallas_call(
        _nm_matmul_kernel,
        grid=(n // tn, bsz, tt // TM),
        in_specs=[pl.BlockSpec((None, TM, d), lambda j, b, i: (b, i, 0)),
                  pl.BlockSpec((None, 3, d), lambda j, b, i: (b * 2 + jnp.where(i >= nct, 1, 0), 0, 0)),
                  pl.BlockSpec((1, d), lambda j, b, i: (0, 0)),
                  pl.BlockSpec((d, tn), lambda j, b, i: (0, j)),
                  pl.BlockSpec((1, tn), lambda j, b, i: (0, j))],
        out_specs=pl.BlockSpec((None, TM, tn), lambda j, b, i: (b, i, j)),
        out_shape=jax.ShapeDtypeStruct((bsz, tt, n), F32),
        compiler_params=_cp(("arbitrary", "arbitrary", "arbitrary")),
    )(xs, mods, g, w, bias)


def _lru_out_kernel(x_ref, mod_ref, y0_ref, y1_ref, z_ref, w_ref, o_ref):
    y = (y0_ref[...] + y1_ref[...]) * _silu(z_ref[...])
    o_ref[...] = x_ref[...] + mod_ref[2:3, :] * _bdot(y, w_ref[...])


def _mlstm_out_kernel(x_ref, mod_ref, h0_ref, h1_ref, z_ref, ng_ref, w_ref, o_ref, hn_s):
    hs = h0_ref[...] + h1_ref[...]
    for hd in range(MLSTM_HEADS):
        sl = slice(hd * MLSTM_DV, (hd + 1) * MLSTM_DV)
        hh = hs[:, sl]
        hn_s[:, sl] = hh * lax.rsqrt(jnp.mean(hh * hh, axis=-1, keepdims=True) + NORM_EPS)
    y = (hn_s[...] * ng_ref[...]) * _silu(z_ref[...])
    o_ref[...] = x_ref[...] + mod_ref[2:3, :] * _bdot(y, w_ref[...])


def _rwkv_out_kernel(x_ref, mod_ref, y0_ref, y1_ref, r_ref, k0_ref, k1_ref, v_ref, z_ref,
                     rk_ref, lg_ref, lb_ref, hm_ref, w_ref, o_ref):
    hm = hm_ref[...]
    y = y0_ref[...] + y1_ref[...]
    mean = _hdot(y, hm)
    yc = y - mean
    var = _hdot(yc * yc, hm)
    yn = yc * lax.rsqrt(var + GN_EPS) * lg_ref[...] + lb_ref[...]
    rv = r_ref[...]
    bonus = _hdot(rv * k0_ref[...] * rk_ref[...] + rv * k1_ref[...] * rk_ref[...], hm) * float(RWKV_HEAD)
    out = (yn + bonus * v_ref[...]) * _silu(z_ref[...])
    o_ref[...] = x_ref[...] + mod_ref[2:3, :] * _bdot(out, w_ref[...])


def _lru_scan_kernel(u_ref, up_ref, un_ref, cw_ref, cb_ref, gw_ref, gb_ref, lam_ref, y_ref,
                     ext, a_s, b_s, h_s, *, nt, nct, reverse):
    s = pl.program_id(1)
    tile = _tile_of_step(s, nt, nct, reverse)
    seg_first = jnp.logical_or(tile == 0, tile == nct)
    seg_last = jnp.logical_or(tile == nct - 1, tile == nt - 1)
    halo = SUBLANES

    @pl.when(s == 0)
    def _():
        h_s[...] = jnp.zeros_like(h_s)

    ext[0:halo, :] = jnp.where(seg_first, 0.0, up_ref[...])
    ext[halo:halo + TM, :] = u_ref[...]
    ext[halo + TM:2 * halo + TM, :] = jnp.where(seg_last, 0.0, un_ref[...])
    left = CONV_W // 2
    cv = cb_ref[...]
    for j in range(CONV_W):
        cv = cv + ext[pl.ds(halo - left + j, TM), :] * cw_ref[j:j + 1, :]

    nblk = cv.shape[1] // LRU_BS
    neg_c_sp = -LRU_C * _softplus(-lam_ref[...])
    for n in range(nblk):
        sl = slice(n * LRU_BS, (n + 1) * LRU_BS)
        blk = cv[:, sl]
        gates = _bdot(blk, gw_ref[n])
        r = jax.nn.sigmoid(gates[:, :LRU_BS] + gb_ref[0:1, sl])
        i = jax.nn.sigmoid(gates[:, LRU_BS:] + gb_ref[1:2, sl])
        log_a = neg_c_sp[:, sl] * r
        a = jnp.exp(log_a)
        a_s[:, sl] = a
        b_s[:, sl] = jnp.sqrt(-jnp.tanh(log_a) * (a * a + 1.0)) * (i * blk)

    def body(j, h):
        t = TM - 1 - j if reverse else j
        h = a_s[pl.ds(t, 1), :] * h + b_s[pl.ds(t, 1), :]
        y_ref[pl.ds(t, 1), :] = h
        return h

    h_s[...] = lax.fori_loop(0, TM, body, h_s[...], unroll=8)


def _lru_scan(uz, conv_w, conv_b, gw, gb, lam, nct, reverse):
    bsz, tt, e2 = uz.shape
    e = e2 // 2
    nt = tt // TM
    r8 = TM // SUBLANES
    nb8 = tt // SUBLANES
    tile = lambda s: _tile_of_step(s, nt, nct, reverse)
    kern = functools.partial(_lru_scan_kernel, nt=nt, nct=nct, reverse=reverse)
    return pl.pallas_call(
        kern,
        grid=(bsz, nt),
        in_specs=[pl.BlockSpec((None, TM, e), lambda b, s: (b, tile(s), 0)),
                  pl.BlockSpec((None, SUBLANES, e), lambda b, s: (b, jnp.maximum(tile(s) * r8 - 1, 0), 0)),
                  pl.BlockSpec((None, SUBLANES, e), lambda b, s: (b, jnp.minimum((tile(s) + 1) * r8, nb8 - 1), 0)),
                  pl.BlockSpec((CONV_W, e), lambda b, s: (0, 0)),
                  pl.BlockSpec((1, e), lambda b, s: (0, 0)),
                  pl.BlockSpec(gw.shape, lambda b, s: (0, 0, 0)),
                  pl.BlockSpec((2, e), lambda b, s: (0, 0)),
                  pl.BlockSpec((1, e), lambda b, s: (0, 0))],
        out_specs=pl.BlockSpec((None, TM, e), lambda b, s: (b, tile(s), 0)),
        out_shape=jax.ShapeDtypeStruct((bsz, tt, e), F32),
        scratch_shapes=[pltpu.VMEM((TM + 2 * SUBLANES, e), F32),
                        pltpu.VMEM((TM, e), F32),
                        pltpu.VMEM((TM, e), F32),
                        pltpu.VMEM((1, e), F32)],
        compiler_params=_cp(("arbitrary", "arbitrary")),
    )(uz, uz, uz, conv_w, conv_b, gw, gb, lam)


def _lru_layer(xs, mods, norm_g, w_in, conv_w, conv_b, gate_w, gate_b, lam, w_out, nct):
    bsz, tt, d = xs.shape
    e = w_out.shape[0]
    nblk = e // LRU_BS
    uz = _nm_matmul(xs, mods, norm_g.reshape(1, d), w_in.astype(BF16),
                    jnp.zeros((1, 2 * e), F32), e, nct)
    ys = []
    for dr in range(2):
        gw = jnp.transpose(gate_w[dr], (1, 2, 0, 3)).reshape(nblk, LRU_BS, 2 * LRU_BS).astype(BF16)
        ys.append(_lru_scan(uz, conv_w, conv_b.reshape(1, e), gw, gate_b[dr], lam[dr].reshape(1, e),
                            nct, reverse=(dr == 1)))
    return pl.pallas_call(
        _lru_out_kernel,
        grid=(bsz, tt // TM),
        in_specs=[pl.BlockSpec((None, TM, d), lambda b, i: (b, i, 0)),
                  pl.BlockSpec((None, 3, d), _mod_index(nct)),
                  pl.BlockSpec((None, TM, e), lambda b, i: (b, i, 0)),
                  pl.BlockSpec((None, TM, e), lambda b, i: (b, i, 0)),
                  pl.BlockSpec((None, TM, e), lambda b, i: (b, i, 1)),
                  pl.BlockSpec((e, d), lambda b, i: (0, 0))],
        out_specs=pl.BlockSpec((None, TM, d), lambda b, i: (b, i, 0)),
        out_shape=jax.ShapeDtypeStruct(xs.shape, F32),
        compiler_params=_cp(("arbitrary", "arbitrary")),
    )(xs, mods, ys[0], ys[1], uz, w_out.astype(BF16))


def _mlstm_scan_kernel(q_ref, k_ref, v_ref, g_ref, it_ref, ft_ref, o_ref, c_s, n_s, m_s,
                       *, dr, reverse):
    hd = pl.program_id(1)
    s = pl.program_id(2)
    L = MLSTM_CHUNK
    nchunk = TM // L

    @pl.when(s == 0)
    def _():
        c_s[...] = jnp.zeros_like(c_s)
        n_s[...] = jnp.zeros_like(n_s)
        m_s[...] = jnp.zeros_like(m_s)

    row = lax.broadcasted_iota(jnp.int32, (L, L), 0)
    col = lax.broadcasted_iota(jnp.int32, (L, L), 1)
    causal = (col >= row) if reverse else (col <= row)
    causal_t = (row >= col) if reverse else (row <= col)
    lane = lax.broadcasted_iota(jnp.int32, (L, LANES), 1)
    i_lane = dr * 2 * MLSTM_HEADS + hd
    f_lane = i_lane + MLSTM_HEADS
    last = 0 if reverse else L - 1

    def chunk(j, carry):
        c = nchunk - 1 - j if reverse else j
        r0 = pl.multiple_of(c * L, L)
        q = q_ref[pl.ds(r0, L), :]
        k = k_ref[pl.ds(r0, L), :] * (MLSTM_DK ** -0.5)
        v = v_ref[pl.ds(r0, L), :]
        g = g_ref[pl.ds(r0, L), :]
        i_col = jnp.sum(jnp.where(lane == i_lane, g, 0.0), axis=1, keepdims=True)
        f_col = _log_sigmoid(jnp.sum(jnp.where(lane == f_lane, g, 0.0), axis=1, keepdims=True))
        i_row = it_ref[pl.ds(c, 1), :]
        f_row = _log_sigmoid(ft_ref[pl.ds(c, 1), :])
        bcum_col = jnp.sum(jnp.where(causal, f_row, 0.0), axis=1, keepdims=True)
        bcum_row = jnp.sum(jnp.where(causal_t, f_col, 0.0), axis=0, keepdims=True)
        logw = jnp.where(causal, bcum_col - bcum_row + i_row, -jnp.inf)
        m_prev = m_s[...]
        gq = bcum_col + m_prev
        m_t = jnp.maximum(gq, jnp.max(logw, axis=1, keepdims=True))
        sm = _dot_nt(q.astype(BF16), k.astype(BF16)) * jnp.exp(logw - m_t)
        inter = jnp.exp(gq - m_t)
        num = _bdot(sm, v) + inter * _bdot(q, c_s[...])
        den = jnp.sum(sm, axis=1, keepdims=True) + inter * jnp.sum(q * n_s[...], axis=1, keepdims=True)
        o_ref[pl.ds(r0, L), :] = num / jnp.maximum(jnp.abs(den), jnp.exp(-m_t))
        m_new = m_t[last:last + 1, :]
        b_last = bcum_col[last:last + 1, :]
        wgt = jnp.exp(b_last - bcum_col + i_col - m_new)
        decay = jnp.exp(b_last + m_prev - m_new)
        kw = k * wgt
        c_s[...] = decay * c_s[...] + _dot_tn(kw.astype(BF16), v.astype(BF16))
        n_s[...] = decay * n_s[...] + jnp.sum(kw, axis=0, keepdims=True)
        m_s[...] = m_new
        return carry

    lax.fori_loop(0, nchunk, chunk, 0)


def _mlstm_scan(qkvz, g, gt, nct, dr):
    bsz, tt, _ = qkvz.shape
    nt = tt // TM
    reverse = dr == 1
    nh = MLSTM_HEADS
    nchunk = TM // MLSTM_CHUNK
    tile = lambda s: _tile_of_step(s, nt, nct, reverse)
    kern = functools.partial(_mlstm_scan_kernel, dr=dr, reverse=reverse)
    koff = nh
    voff = 2 * nh * MLSTM_DK // MLSTM_DV
    return pl.pallas_call(
        kern,
        grid=(bsz, nh, nt),
        in_specs=[pl.BlockSpec((None, TM, MLSTM_DK), lambda b, h, s: (b, tile(s), h)),
                  pl.BlockSpec((None, TM, MLSTM_DK), lambda b, h, s: (b, tile(s), koff + h)),
                  pl.BlockSpec((None, TM, MLSTM_DV), lambda b, h, s: (b, tile(s), voff + h)),
                  pl.BlockSpec((None, TM, LANES), lambda b, h, s: (b, tile(s), 0)),
                  pl.BlockSpec((None, None, None, nchunk, MLSTM_CHUNK),
                               lambda b, h, s: (b, dr * 2 * nh + h, tile(s), 0, 0)),
                  pl.BlockSpec((None, None, None, nchunk, MLSTM_CHUNK),
                               lambda b, h, s: (b, dr * 2 * nh + nh + h, tile(s), 0, 0))],
        out_specs=pl.BlockSpec((None, TM, MLSTM_DV), lambda b, h, s: (b, tile(s), h)),
        out_shape=jax.ShapeDtypeStruct((bsz, tt, nh * MLSTM_DV), F32),
        scratch_shapes=[pltpu.VMEM((MLSTM_DK, MLSTM_DV), F32),
                        pltpu.VMEM((1, MLSTM_DK), F32),
                        pltpu.VMEM((1, 1), F32)],
        compiler_params=_cp(("arbitrary", "arbitrary", "arbitrary")),
    )(qkvz, qkvz, qkvz, g, gt, gt)


def _mlstm_layer(xs, mods, norm_g, w_in, gate_b, hnorm_g, w_out, nct):
    bsz, tt, d = xs.shape
    nh = MLSTM_HEADS
    qk = nh * MLSTM_DK
    width = nh * MLSTM_DV
    nmain = 2 * qk + 2 * width
    ng = 4 * nh
    g2 = norm_g.reshape(1, d)
    qkvz = _nm_matmul(xs, mods, g2, w_in[:, :nmain].astype(BF16), jnp.zeros((1, nmain), F32),
                      width, nct)
    wg = jnp.pad(w_in[:, nmain:], ((0, 0), (0, LANES - ng))).astype(BF16)
    bg = jnp.pad(gate_b.reshape(1, ng), ((0, 0), (0, LANES - ng)))
    g = _nm_matmul(xs, mods, g2, wg, bg, LANES, nct)
    gt = jnp.transpose(g[:, :, :ng], (0, 2, 1)).reshape(bsz, ng, tt // TM, TM // MLSTM_CHUNK, MLSTM_CHUNK)
    hs = [_mlstm_scan(qkvz, g, gt, nct, dr) for dr in range(2)]
    return pl.pallas_call(
        _mlstm_out_kernel,
        grid=(bsz, tt // TM),
        in_specs=[pl.BlockSpec((None, TM, d), lambda b, i: (b, i, 0)),
                  pl.BlockSpec((None, 3, d), _mod_index(nct)),
                  pl.BlockSpec((None, TM, width), lambda b, i: (b, i, 0)),
                  pl.BlockSpec((None, TM, width), lambda b, i: (b, i, 0)),
                  pl.BlockSpec((None, TM, width), lambda b, i: (b, i, (2 * qk + width) // width)),
                  pl.BlockSpec((1, width), lambda b, i: (0, 0)),
                  pl.BlockSpec((width, d), lambda b, i: (0, 0))],
        out_specs=pl.BlockSpec((None, TM, d), lambda b, i: (b, i, 0)),
        out_shape=jax.ShapeDtypeStruct(xs.shape, F32),
        scratch_shapes=[pltpu.VMEM((TM, width), F32)],
        compiler_params=_cp(("arbitrary", "arbitrary")),
    )(xs, mods, hs[0], hs[1], qkvz, hnorm_g.reshape(1, width), w_out.astype(BF16))


def _rwkv_prep_kernel(x_ref, xp_ref, xn_ref, mod_ref, g_ref, mu_ref, wr_ref, w0_ref, w1_ref, w2_ref,
                      a0_ref, a1_ref, a2_ref, kk_ref, ka_ref, hm_ref,
                      r_o, kn_o, v_o, z_o, lw0_o, lw1_o, be0_o, be1_o, k0_o, k1_o,
                      hs, sh, *, nt, nct):
    i = pl.program_id(1)
    seg_first = jnp.logical_or(i == 0, i == nct)
    seg_last = jnp.logical_or(i == nct - 1, i == nt - 1)
    gw = GRID_W
    d = x_ref.shape[1]
    g = g_ref[...]
    hs[0:gw, :] = jnp.where(seg_first, 0.0, _normmod(xp_ref[...], g, mod_ref))
    hs[gw:gw + TM, :] = _normmod(x_ref[...], g, mod_ref)
    hs[gw + TM:2 * gw + TM, :] = jnp.where(seg_last, 0.0, _normmod(xn_ref[...], g, mod_ref))

    @pl.when(i < nct)
    def _():
        half = d // 2
        sh[:, 0:half] = hs[pl.ds(gw - 1, TM), 0:half]
        sh[:, half:d] = hs[pl.ds(gw + 1, TM), half:d]

    @pl.when(i >= nct)
    def _():
        q = d // 4
        wcol = lax.broadcasted_iota(jnp.int32, (TM, q), 0) % gw
        sh[:, 0:q] = jnp.where(wcol == 0, 0.0, hs[pl.ds(gw - 1, TM), 0:q])
        sh[:, q:2 * q] = jnp.where(wcol == gw - 1, 0.0, hs[pl.ds(gw + 1, TM), q:2 * q])
        sh[:, 2 * q:3 * q] = hs[pl.ds(0, TM), 2 * q:3 * q]
        sh[:, 3 * q:d] = hs[pl.ds(2 * gw, TM), 3 * q:d]

    h = hs[pl.ds(gw, TM), :]
    dx = sh[...] - h
    mix = lambda n: h + dx * mu_ref[n:n + 1, :]
    hm = hm_ref[...]
    r = _bdot(mix(0), wr_ref[0])
    k = _bdot(mix(1), wr_ref[1])
    r_o[...] = r
    v_o[...] = _bdot(mix(2), wr_ref[2])
    z_o[...] = _bdot(mix(3), wr_ref[3])
    kk = k * kk_ref[...]
    ss = _hdot(kk * kk, hm) * float(RWKV_HEAD)
    kn = kk / jnp.maximum(jnp.sqrt(ss), 1e-12)
    kn_o[...] = kn
    xw = mix(4)
    xa = mix(5)
    for dr, (lw_o, be_o, k_o) in enumerate(((lw0_o, be0_o, k0_o), (lw1_o, be1_o, k1_o))):
        w_pre = w0_ref[dr:dr + 1, :] + _bdot(jnp.tanh(_bdot(xw, w1_ref[dr])), w2_ref[dr])
        lw_o[...] = -jnp.exp(-_softplus(-w_pre) - 0.5)
        a = jax.nn.sigmoid(a0_ref[dr:dr + 1, :] + _bdot(_bdot(xa, a1_ref[dr]), a2_ref[dr]))
        be_o[...] = a * kn
        k_o[...] = k * (1.0 + (a - 1.0) * ka_ref[...])


def _chunk_cumsum(x, chunk, reverse):
    n = x.shape[0]
    pos = lax.broadcasted_iota(jnp.int32, x.shape, 0) % chunk
    sh = 1
    while sh < chunk:
        if reverse:
            x = x + jnp.where(pos < chunk - sh, pltpu.roll(x, n - sh, 0), 0.0)
        else:
            x = x + jnp.where(pos >= sh, pltpu.roll(x, sh, 0), 0.0)
        sh *= 2
    return x


def _unit_lower_inverses(ns, strict, eye, blk):
    L = ns[0].shape[0]
    row = lax.broadcasted_iota(jnp.int32, (L, L), 0)
    col = lax.broadcasted_iota(jnp.int32, (L, L), 1)
    same = (row // blk) == (col // blk)
    in_blk = jnp.logical_and(strict, same)
    off_blk = jnp.logical_and(strict, jnp.logical_not(same))
    pws = [jnp.where(in_blk, n, 0.0) for n in ns]
    tds = [eye - p for p in pws]
    for _ in range(blk.bit_length() - 2):
        pws = [_bdot(p, p) for p in pws]
        tds = [_bdot(t, eye + p) for t, p in zip(tds, pws)]
    pws = [_bdot(t, jnp.where(off_blk, n, 0.0)) for t, n in zip(tds, ns)]
    tis = [eye - p for p in pws]
    for _ in range((L // blk).bit_length() - 2):
        pws = [_bdot(p, p) for p in pws]
        tis = [_bdot(t, eye + p) for t, p in zip(tis, pws)]
    return [_bdot(ti, td) for ti, td in zip(tis, tds)]


def _rwkv_scan_kernel(r_ref, kn_ref, v_ref, lw_ref, be_ref, kd_ref, y_ref,
                      st_s, rp_s, y0_s, m_s, z_s, pl_s, *, reverse):
    s = pl.program_id(2)
    L = RWKV_CHUNK
    N = RWKV_HEAD
    nchunk = TM // L
    nhead = r_ref.shape[1] // N

    @pl.when(s == 0)
    def _():
        st_s[...] = jnp.zeros_like(st_s)

    row = lax.broadcasted_iota(jnp.int32, (L, L), 0)
    col = lax.broadcasted_iota(jnp.int32, (L, L), 1)
    incl = (col >= row) if reverse else (col <= row)
    strict = (col > row) if reverse else (col < row)
    eye = (col == row).astype(F32)
    last = 0 if reverse else L - 1

    lw_all = lw_ref[...]
    lc_all = _chunk_cumsum(lw_all, L, reverse)

    a_p, r_p, b_i, k_i, b_l, k_l, vh, where = [], [], [], [], [], [], [], []
    for c in range(nchunk):
        rows = slice(c * L, (c + 1) * L)
        lw = lw_all[rows]
        lc = lc_all[rows]
        lc_last = lc[last:last + 1, :]
        einv = jnp.exp(-lc)
        to_end = jnp.exp(lc_last - lc)
        be = be_ref[rows, :]
        kd = kd_ref[rows, :]
        full = [(kn_ref[rows, :] * jnp.exp(lc - lw)).astype(BF16),
                (r_ref[rows, :] * jnp.exp(lc)).astype(BF16),
                (be * einv).astype(BF16), (kd * einv).astype(BF16),
                (be * to_end).astype(BF16), (kd * to_end).astype(BF16),
                v_ref[rows, :].astype(BF16)]
        pl_s[c] = jnp.exp(lc_last)
        for hh in range(nhead):
            sl = slice(hh * N, (hh + 1) * N)
            for dst, src in zip((a_p, r_p, b_i, k_i, b_l, k_l, vh), full):
                dst.append(src[:, sl])
            where.append((c, rows, hh, sl))
    n_inst = len(where)
    ams = [_dot_nt(jnp.concatenate([a_p[i], r_p[i]], axis=0),
                   jnp.concatenate([b_i[i], k_i[i]], axis=0)) for i in range(n_inst)]
    tinvs = _unit_lower_inverses([am[:L, :L] for am in ams], strict, eye, 16)
    avs = [_bdot(jnp.where(strict, ams[i][:L, L:], 0.0), vh[i]) for i in range(n_inst)]
    xs = [_bdot(tinvs[i], jnp.concatenate([a_p[i].astype(F32), avs[i]], axis=1)) for i in range(n_inst)]
    rks = [_bdot(jnp.where(incl, ams[i][L:, L:], 0.0), vh[i]) for i in range(n_inst)]
    rbs = [_bdot(jnp.where(incl, ams[i][L:, :L], 0.0), xs[i]) for i in range(n_inst)]
    mzs = [_dot_tn(xs[i].astype(BF16), b_l[i]) for i in range(n_inst)]
    vks = [_dot_tn(vh[i], k_l[i]) for i in range(n_inst)]
    for i, (c, rows, hh, sl) in enumerate(where):
        rp_s[rows, sl] = r_p[i].astype(F32) - rbs[i][:, :N]
        y0_s[rows, sl] = rks[i] - rbs[i][:, N:]
        m_s[c, hh] = -mzs[i][:N]
        z_s[c, hh] = vks[i] - mzs[i][N:]

    for j in range(nchunk):
        c = nchunk - 1 - j if reverse else j
        rows = slice(c * L, (c + 1) * L)
        p_l = pl_s[c]
        for hh in range(nhead):
            sl = slice(hh * N, (hh + 1) * N)
            st = st_s[hh]
            st16 = st.astype(BF16)
            y_ref[rows, sl] = _dot_nt(rp_s[rows, sl].astype(BF16), st16) + y0_s[rows, sl]
            st_s[hh] = st * p_l[:, sl] + _bdot(st16, m_s[c, hh]) + z_s[c, hh]


def _rwkv_scan(r, kn, v, lw, be, kd, nct, reverse):
    bsz, tt, w = r.shape
    nt = tt // TM
    tile = lambda s: _tile_of_step(s, nt, nct, reverse)
    spec = pl.BlockSpec((None, TM, LANES), lambda b, p, s: (b, tile(s), p))
    kern = functools.partial(_rwkv_scan_kernel, reverse=reverse)
    nh = LANES // RWKV_HEAD
    nchunk = TM // RWKV_CHUNK
    return pl.pallas_call(
        kern,
        grid=(bsz, w // LANES, nt),
        in_specs=[spec] * 6,
        out_specs=spec,
        out_shape=jax.ShapeDtypeStruct((bsz, tt, w), F32),
        scratch_shapes=[pltpu.VMEM((nh, RWKV_HEAD, RWKV_HEAD), F32),
                        pltpu.VMEM((TM, LANES), F32),
                        pltpu.VMEM((TM, LANES), F32),
                        pltpu.VMEM((nchunk, nh, RWKV_HEAD, RWKV_HEAD), F32),
                        pltpu.VMEM((nchunk, nh, RWKV_HEAD, RWKV_HEAD), F32),
                        pltpu.VMEM((nchunk, 1, LANES), F32)],
        compiler_params=_cp(("arbitrary", "arbitrary", "arbitrary")),
    )(r, kn, v, lw, be, kd)


def _rwkv_layer(xs, mods, norm_g, mu, w_rkvz, w0, w1, w2, a0, a1, a2, k_k, k_a, r_k, ln_g, ln_b,
                w_out, nct):
    bsz, tt, d = xs.shape
    nt = tt // TM
    r64 = TM // GRID_W
    nb64 = tt // GRID_W
    hidx = jnp.arange(d) // RWKV_HEAD
    hmean = (hidx[:, None] == hidx[None, :]).astype(F32) / RWKV_HEAD
    row = lambda a: a.reshape(1, d)
    full = lambda a: pl.BlockSpec(a.shape, lambda b, i: (0,) * a.ndim)
    tile_spec = pl.BlockSpec((None, TM, d), lambda b, i: (b, i, 0))
    consts = [row(norm_g), mu, w_rkvz.astype(BF16), w0, w1.astype(BF16), w2.astype(BF16),
              a0, a1.astype(BF16), a2.astype(BF16), row(k_k), row(k_a), hmean]
    kern = functools.partial(_rwkv_prep_kernel, nt=nt, nct=nct)
    outs = pl.pallas_call(
        kern,
        grid=(bsz, nt),
        in_specs=[tile_spec,
                  pl.BlockSpec((None, GRID_W, d), lambda b, i: (b, jnp.maximum(i * r64 - 1, 0), 0)),
                  pl.BlockSpec((None, GRID_W, d), lambda b, i: (b, jnp.minimum((i + 1) * r64, nb64 - 1), 0)),
                  pl.BlockSpec((None, 3, d), _mod_index(nct))] + [full(a) for a in consts],
        out_specs=[tile_spec] * 10,
        out_shape=[jax.ShapeDtypeStruct(xs.shape, F32)] * 10,
        scratch_shapes=[pltpu.VMEM((TM + 2 * GRID_W, d), F32), pltpu.VMEM((TM, d), F32)],
        compiler_params=_cp(("arbitrary", "arbitrary")),
    )(xs, xs, xs, mods, *consts)
    r, kn, v, z, lw0, lw1, be0, be1, k0, k1 = outs
    y0 = _rwkv_scan(r, kn, v, lw0, be0, k0, nct, reverse=False)
    y1 = _rwkv_scan(r, kn, v, lw1, be1, k1, nct, reverse=True)
    fin = [row(r_k), row(ln_g), row(ln_b), hmean, w_out.astype(BF16)]
    return pl.pallas_call(
        _rwkv_out_kernel,
        grid=(bsz, nt),
        in_specs=[tile_spec, pl.BlockSpec((None, 3, d), _mod_index(nct))] + [tile_spec] * 7
                 + [full(a) for a in fin],
        out_specs=tile_spec,
        out_shape=jax.ShapeDtypeStruct(xs.shape, F32),
        compiler_params=_cp(("arbitrary", "arbitrary")),
    )(xs, mods, y0, y1, r, k0, k1, v, z, *fin)


def _final_norm_kernel(x_ref, g_ref, o_ref):
    xv = x_ref[...]
    o_ref[...] = xv * lax.rsqrt(jnp.mean(xv * xv, axis=-1, keepdims=True) + NORM_EPS) * g_ref[...]


def _final_norm(xs, g, nct):
    bsz, tt, d = xs.shape
    t = tt - nct * TM
    return pl.pallas_call(
        _final_norm_kernel,
        grid=(bsz, t // TM),
        in_specs=[pl.BlockSpec((None, TM, d), lambda b, i: (b, i + nct, 0)),
                  pl.BlockSpec((1, d), lambda b, i: (0, 0))],
        out_specs=pl.BlockSpec((None, TM, d), lambda b, i: (b, i, 0)),
        out_shape=jax.ShapeDtypeStruct((bsz, t, d), F32),
        compiler_params=_cp(("arbitrary", "arbitrary")),
    )(xs, g.reshape(1, d))


def kernel(x, c, ctx, c_ctx, norm_g, mod_w, mod_b, final_g, lru_w_in, lru_conv_w, lru_conv_b, lru_gate_w, lru_gate_b, lru_lam, lru_w_out, mlstm_w_in, mlstm_gate_b, mlstm_norm_g, mlstm_w_out, r7_mu, r7_w_rkvz, r7_w0, r7_w1, r7_w2, r7_a0, r7_a1, r7_a2, r7_k_k, r7_k_a, r7_r_k, r7_ln_g, r7_ln_b, r7_w_out):
    bsz, t, d = x.shape
    ctx_len = ctx.shape[1]
    depth = norm_g.shape[0]
    assert ctx_len % TM == 0 and t % TM == 0 and t % GRID_W == 0 and bsz < SUBLANES
    nct = ctx_len // TM

    cc = jnp.zeros((SUBLANES, d), F32).at[:bsz].set(c).at[bsz].set(c_ctx)
    mp = _mod_params(cc, mod_w, mod_b).reshape(depth, SUBLANES, 3, d)
    mods_all = jnp.stack([jnp.broadcast_to(mp[:, bsz][:, None], (depth, bsz, 3, d)), mp[:, :bsz]],
                         axis=2).reshape(depth, 2 * bsz, 3, d)

    xs = jnp.concatenate([ctx, x], axis=1)
    for i in range(depth):
        kind, j = i % 3, i // 3
        mods = mods_all[i]
        if kind == 0:
            xs = _lru_layer(xs, mods, norm_g[i], lru_w_in[j], lru_conv_w[j], lru_conv_b[j],
                            lru_gate_w[j], lru_gate_b[j], lru_lam[j], lru_w_out[j], nct)
        elif kind == 1:
            xs = _mlstm_layer(xs, mods, norm_g[i], mlstm_w_in[j], mlstm_gate_b[j], mlstm_norm_g[j],
                              mlstm_w_out[j], nct)
        else:
            xs = _rwkv_layer(xs, mods, norm_g[i], r7_mu[j], r7_w_rkvz[j], r7_w0[j], r7_w1[j], r7_w2[j],
                             r7_a0[j], r7_a1[j], r7_a2[j], r7_k_k[j], r7_k_a[j], r7_r_k[j],
                             r7_ln_g[j], r7_ln_b[j], r7_w_out[j], nct)
    return _final_norm(xs, final_g, nct)
```

```python
import functools

import jax
import jax.numpy as jnp
from jax import lax
from jax.experimental import pallas as pl
from jax.experimental.pallas import tpu as pltpu

F32 = jnp.float32
BF16 = jnp.bfloat16
HI = lax.Precision.HIGHEST

GRID_W = 64
LRU_C = 8.0
LRU_BS = 128
CONV_W = 4
MLSTM_HEADS = 8
MLSTM_DK = 128
MLSTM_DV = 256
MLSTM_CHUNK = 64
RWKV_HEAD = 64
RWKV_CHUNK = 64
GN_EPS = 64e-5
NORM_EPS = 1e-6

TM = 256
SUBLANES = 8
LANES = 128
VMEM_LIMIT = 56 * 1024 * 1024


def _cp(sem):
    return pltpu.CompilerParams(dimension_semantics=sem, vmem_limit_bytes=VMEM_LIMIT)


def _silu(v):
    return v * jax.nn.sigmoid(v)


def _softplus(v):
    return jnp.maximum(v, 0.0) + jnp.log1p(jnp.exp(-jnp.abs(v)))


def _log_sigmoid(v):
    return jnp.minimum(v, 0.0) - jnp.log1p(jnp.exp(-jnp.abs(v)))


def _bdot(a, b):
    return jnp.dot(a.astype(BF16), b.astype(BF16), preferred_element_type=F32)


def _hdot(a, b):
    return jnp.dot(a, b, precision=HI, preferred_element_type=F32)


def _dot_nt(a, b, precision=None):
    return lax.dot_general(a, b, (((1,), (1,)), ((), ())), precision=precision,
                           preferred_element_type=F32)


def _dot_tn(a, b, precision=None):
    return lax.dot_general(a, b, (((0,), (0,)), ((), ())), precision=precision,
                           preferred_element_type=F32)


def _normmod(xv, g, mod_ref):
    y = xv * lax.rsqrt(jnp.mean(xv * xv, axis=-1, keepdims=True) + NORM_EPS)
    return (y * g) * (1.0 + mod_ref[1:2, :]) + mod_ref[0:1, :]


def _tile_of_step(s, nt, nct, reverse):
    if not reverse:
        return s
    return jnp.where(s < nct, nct - 1 - s, nt - 1 - (s - nct))


def _mod_index(nct):
    return lambda b, i: (b * 2 + jnp.where(i >= nct, 1, 0), 0, 0)


def _mod_kernel(c_ref, w_ref, b_ref, o_ref):
    o_ref[...] = _hdot(_silu(c_ref[...]), w_ref[...]) + b_ref[...]


def _mod_params(cc, mod_w, mod_b):
    depth, d, d3 = mod_w.shape
    rows = cc.shape[0]
    return pl.pallas_call(
        _mod_kernel,
        grid=(depth, d3 // d),
        in_specs=[pl.BlockSpec((rows, d), lambda l, j: (0, 0)),
                  pl.BlockSpec((None, d, d), lambda l, j: (l, 0, j)),
                  pl.BlockSpec((None, 1, d), lambda l, j: (l, 0, j))],
        out_specs=pl.BlockSpec((None, rows, d), lambda l, j: (l, 0, j)),
        out_shape=jax.ShapeDtypeStruct((depth, rows, d3), F32),
        compiler_params=_cp(("arbitrary", "arbitrary")),
    )(cc, mod_w, mod_b.reshape(depth, 1, d3))


def _nm_matmul_kernel(x_ref, mod_ref, g_ref, w_ref, b_ref, o_ref):
    h = _normmod(x_ref[...], g_ref[...], mod_ref)
    o_ref[...] = _bdot(h, w_ref[...]) + b_ref[...]


def _nm_matmul(xs, mods, g, w, bias, tn, nct):
    bsz, tt, d = xs.shape
    n = w.shape[1]
    return pl.pallas_call(
        _nm_matmul_kernel,
        grid=(n // tn, bsz, tt // TM),
        in_specs=[pl.BlockSpec((None, TM, d), lambda j, b, i: (b, i, 0)),
                  pl.BlockSpec((None, 3, d), lambda j, b, i: (b * 2 + jnp.where(i >= nct, 1, 0), 0, 0)),
                  pl.BlockSpec((1, d), lambda j, b, i: (0, 0)),
                  pl.BlockSpec((d, tn), lambda j, b, i: (0, j)),
                  pl.BlockSpec((1, tn), lambda j, b, i: (0, j))],
        out_specs=pl.BlockSpec((None, TM, tn), lambda j, b, i: (b, i, j)),
        out_shape=jax.ShapeDtypeStruct((bsz, tt, n), F32),
        compiler_params=_cp(("arbitrary", "arbitrary", "arbitrary")),
    )(xs, mods, g, w, bias)


def _lru_out_kernel(x_ref, mod_ref, y0_ref, y1_ref, z_ref, w_ref, o_ref):
    y = (y0_ref[...] + y1_ref[...]) * _silu(z_ref[...])
    o_ref[...] = x_ref[...] + mod_ref[2:3, :] * _bdot(y, w_ref[...])


def _mlstm_out_kernel(x_ref, mod_ref, h0_ref, h1_ref, z_ref, ng_ref, w_ref, o_ref, hn_s):
    hs = h0_ref[...] + h1_ref[...]
    for hd in range(MLSTM_HEADS):
        sl = slice(hd * MLSTM_DV, (hd + 1) * MLSTM_DV)
        hh = hs[:, sl]
        hn_s[:, sl] = hh * lax.rsqrt(jnp.mean(hh * hh, axis=-1, keepdims=True) + NORM_EPS)
    y = (hn_s[...] * ng_ref[...]) * _silu(z_ref[...])
    o_ref[...] = x_ref[...] + mod_ref[2:3, :] * _bdot(y, w_ref[...])


def _split3(v):
    hi = v.astype(BF16)
    rest = v - hi.astype(F32)
    mid = rest.astype(BF16)
    lo = (rest - mid.astype(F32)).astype(BF16)
    return hi, mid, lo


def _head_sum(v, hr_ref, he_ref):
    hr = hr_ref[...]
    he = he_ref[...]
    red = sum(jnp.dot(p, hr, preferred_element_type=F32) for p in _split3(v))
    return sum(jnp.dot(p, he, preferred_element_type=F32) for p in _split3(red))


def _rwkv_out_kernel(x_ref, mod_ref, y0_ref, y1_ref, r_ref, k0_ref, k1_ref, v_ref, z_ref,
                     rk_ref, lg_ref, lb_ref, hr_ref, he_ref, w_ref, o_ref):
    inv_n = 1.0 / RWKV_HEAD
    y = y0_ref[...] + y1_ref[...]
    mean = _head_sum(y, hr_ref, he_ref) * inv_n
    yc = y - mean
    var = _head_sum(yc * yc, hr_ref, he_ref) * inv_n
    yn = yc * lax.rsqrt(var + GN_EPS) * lg_ref[...] + lb_ref[...]
    rv = r_ref[...]
    bonus = _head_sum(rv * k0_ref[...] * rk_ref[...] + rv * k1_ref[...] * rk_ref[...], hr_ref, he_ref)
    out = (yn + bonus * v_ref[...]) * _silu(z_ref[...])
    o_ref[...] = x_ref[...] + mod_ref[2:3, :] * _bdot(out, w_ref[...])


def _lru_scan_kernel(u_ref, up_ref, un_ref, cw_ref, cb_ref, gw_ref, gb_ref, lam_ref, y_ref,
                     ext, a_s, b_s, h_s, *, nt, nct, reverse):
    s = pl.program_id(1)
    tile = _tile_of_step(s, nt, nct, reverse)
    seg_first = jnp.logical_or(tile == 0, tile == nct)
    seg_last = jnp.logical_or(tile == nct - 1, tile == nt - 1)
    halo = SUBLANES

    @pl.when(s == 0)
    def _():
        h_s[...] = jnp.zeros_like(h_s)

    ext[0:halo, :] = jnp.where(seg_first, 0.0, up_ref[...])
    ext[halo:halo + TM, :] = u_ref[...]
    ext[halo + TM:2 * halo + TM, :] = jnp.where(seg_last, 0.0, un_ref[...])
    left = CONV_W // 2
    cv = cb_ref[...]
    for j in range(CONV_W):
        cv = cv + ext[pl.ds(halo - left + j, TM), :] * cw_ref[j:j + 1, :]

    nblk = cv.shape[1] // LRU_BS
    neg_c_sp = -LRU_C * _softplus(-lam_ref[...])
    for n in range(nblk):
        sl = slice(n * LRU_BS, (n + 1) * LRU_BS)
        blk = cv[:, sl]
        gates = _bdot(blk, gw_ref[n])
        r = jax.nn.sigmoid(gates[:, :LRU_BS] + gb_ref[0:1, sl])
        i = jax.nn.sigmoid(gates[:, LRU_BS:] + gb_ref[1:2, sl])
        log_a = neg_c_sp[:, sl] * r
        a = jnp.exp(log_a)
        a_s[:, sl] = a
        b_s[:, sl] = jnp.sqrt(-jnp.tanh(log_a) * (a * a + 1.0)) * (i * blk)

    grp = SUBLANES
    av = a_s[...]
    bv = b_s[...]
    pos = lax.broadcasted_iota(jnp.int32, av.shape, 0) % grp
    sh = 1
    while sh < grp:
        if reverse:
            valid = pos < grp - sh
            a_prev = pltpu.roll(av, TM - sh, 0)
            b_prev = pltpu.roll(bv, TM - sh, 0)
        else:
            valid = pos >= sh
            a_prev = pltpu.roll(av, sh, 0)
            b_prev = pltpu.roll(bv, sh, 0)
        bv = jnp.where(valid, av * b_prev, 0.0) + bv
        av = jnp.where(valid, av * a_prev, av)
        sh *= 2
    a_s[...] = av
    b_s[...] = bv
    ngrp = TM // grp
    edge = 0 if reverse else grp - 1

    def body(j, h):
        r0 = pl.multiple_of((ngrp - 1 - j if reverse else j) * grp, grp)
        hg = a_s[pl.ds(r0, grp), :] * h + b_s[pl.ds(r0, grp), :]
        y_ref[pl.ds(r0, grp), :] = hg
        return hg[edge:edge + 1, :]

    h_s[...] = lax.fori_loop(0, ngrp, body, h_s[...], unroll=4)


def _lru_scan(uz, conv_w, conv_b, gw, gb, lam, nct, reverse):
    bsz, tt, e2 = uz.shape
    e = e2 // 2
    nt = tt // TM
    r8 = TM // SUBLANES
    nb8 = tt // SUBLANES
    tile = lambda s: _tile_of_step(s, nt, nct, reverse)
    kern = functools.partial(_lru_scan_kernel, nt=nt, nct=nct, reverse=reverse)
    return pl.pallas_call(
        kern,
        grid=(bsz, nt),
        in_specs=[pl.BlockSpec((None, TM, e), lambda b, s: (b, tile(s), 0)),
                  pl.BlockSpec((None, SUBLANES, e), lambda b, s: (b, jnp.maximum(tile(s) * r8 - 1, 0), 0)),
                  pl.BlockSpec((None, SUBLANES, e), lambda b, s: (b, jnp.minimum((tile(s) + 1) * r8, nb8 - 1), 0)),
                  pl.BlockSpec((CONV_W, e), lambda b, s: (0, 0)),
                  pl.BlockSpec((1, e), lambda b, s: (0, 0)),
                  pl.BlockSpec(gw.shape, lambda b, s: (0, 0, 0)),
                  pl.BlockSpec((2, e), lambda b, s: (0, 0)),
                  pl.BlockSpec((1, e), lambda b, s: (0, 0))],
        out_specs=pl.BlockSpec((None, TM, e), lambda b, s: (b, tile(s), 0)),
        out_shape=jax.ShapeDtypeStruct((bsz, tt, e), F32),
        scratch_shapes=[pltpu.VMEM((TM + 2 * SUBLANES, e), F32),
                        pltpu.VMEM((TM, e), F32),
                        pltpu.VMEM((TM, e), F32),
                        pltpu.VMEM((1, e), F32)],
        compiler_params=_cp(("arbitrary", "arbitrary")),
    )(uz, uz, uz, conv_w, conv_b, gw, gb, lam)


def _lru_layer(xs, mods, norm_g, w_in, conv_w, conv_b, gate_w, gate_b, lam, w_out, nct):
    bsz, tt, d = xs.shape
    e = w_out.shape[0]
    nblk = e // LRU_BS
    uz = _nm_matmul(xs, mods, norm_g.reshape(1, d), w_in.astype(BF16),
                    jnp.zeros((1, 2 * e), F32), e, nct)
    ys = []
    for dr in range(2):
        gw = jnp.transpose(gate_w[dr], (1, 2, 0, 3)).reshape(nblk, LRU_BS, 2 * LRU_BS).astype(BF16)
        ys.append(_lru_scan(uz, conv_w, conv_b.reshape(1, e), gw, gate_b[dr], lam[dr].reshape(1, e),
                            nct, reverse=(dr == 1)))
    return pl.pallas_call(
        _lru_out_kernel,
        grid=(bsz, tt // TM),
        in_specs=[pl.BlockSpec((None, TM, d), lambda b, i: (b, i, 0)),
                  pl.BlockSpec((None, 3, d), _mod_index(nct)),
                  pl.BlockSpec((None, TM, e), lambda b, i: (b, i, 0)),
                  pl.BlockSpec((None, TM, e), lambda b, i: (b, i, 0)),
                  pl.BlockSpec((None, TM, e), lambda b, i: (b, i, 1)),
                  pl.BlockSpec((e, d), lambda b, i: (0, 0))],
        out_specs=pl.BlockSpec((None, TM, d), lambda b, i: (b, i, 0)),
        out_shape=jax.ShapeDtypeStruct(xs.shape, F32),
        compiler_params=_cp(("arbitrary", "arbitrary")),
    )(xs, mods, ys[0], ys[1], uz, w_out.astype(BF16))


def _mlstm_scan_kernel(q_ref, k_ref, v_ref, g_ref, it_ref, ft_ref, o_ref, c_s, n_s, m_s,
                       *, dr, reverse):
    hg = pl.program_id(1)
    s = pl.program_id(2)
    L = MLSTM_CHUNK
    DK, DV = MLSTM_DK, MLSTM_DV
    nchunk = TM // L
    ng = q_ref.shape[1] // DK

    @pl.when(s == 0)
    def _():
        c_s[...] = jnp.zeros_like(c_s)
        n_s[...] = jnp.zeros_like(n_s)
        m_s[...] = jnp.zeros_like(m_s)

    row = lax.broadcasted_iota(jnp.int32, (L, L), 0)
    col = lax.broadcasted_iota(jnp.int32, (L, L), 1)
    causal = (col >= row) if reverse else (col <= row)
    causal_t = (row >= col) if reverse else (row <= col)
    lane = lax.broadcasted_iota(jnp.int32, (TM, LANES), 1)
    last = 0 if reverse else L - 1
    g = g_ref[...]

    inst = []
    q16, bcum, m_in, pv, d1, b_last, m_loc, kv, ks, qf = [], [], [], [], [], [], [], [], [], []
    for gi in range(ng):
        i_lane = dr * 2 * MLSTM_HEADS + hg * ng + gi
        i_col_t = jnp.sum(jnp.where(lane == i_lane, g, 0.0), axis=1, keepdims=True)
        f_col_t = _log_sigmoid(jnp.sum(jnp.where(lane == i_lane + MLSTM_HEADS, g, 0.0),
                                       axis=1, keepdims=True))
        for c in range(nchunk):
            rows = slice(c * L, (c + 1) * L)
            i_row = it_ref[gi, c:c + 1, :]
            f_row = _log_sigmoid(ft_ref[gi, c:c + 1, :])
            bc = jnp.sum(jnp.where(causal, f_row, 0.0), axis=1, keepdims=True)
            bcum_row = jnp.sum(jnp.where(causal_t, f_col_t[rows], 0.0), axis=0, keepdims=True)
            logw = jnp.where(causal, bc - bcum_row + i_row, -jnp.inf)
            mi = jnp.max(logw, axis=1, keepdims=True)
            q = q_ref[rows, gi * DK:(gi + 1) * DK]
            k = k_ref[rows, gi * DK:(gi + 1) * DK] * (DK ** -0.5)
            v = v_ref[rows, gi * DV:(gi + 1) * DV].astype(BF16)
            sm = _dot_nt(q.astype(BF16), k.astype(BF16)) * jnp.exp(logw - mi)
            bl = bc[last:last + 1, :]
            tot = bl - bc + i_col_t[rows]
            ml = jnp.max(tot, axis=0, keepdims=True)
            kw = k * jnp.exp(tot - ml)
            inst.append((c, gi))
            qf.append(q)
            q16.append(q.astype(BF16))
            bcum.append(bc)
            m_in.append(mi)
            pv.append(_bdot(sm, v))
            d1.append(jnp.sum(sm, axis=1, keepdims=True))
            b_last.append(bl)
            m_loc.append(ml)
            kv.append(_dot_tn(kw.astype(BF16), v))
            ks.append(jnp.sum(kw, axis=0, keepdims=True))

    for j in range(nchunk):
        c = nchunk - 1 - j if reverse else j
        rows = slice(c * L, (c + 1) * L)
        ids = [inst.index((c, gi)) for gi in range(ng)]
        qc = [_bdot(q16[i], c_s[gi]) for gi, i in enumerate(ids)]
        qn = [jnp.sum(qf[i] * n_s[gi], axis=1, keepdims=True) for gi, i in enumerate(ids)]
        for gi, i in enumerate(ids):
            m_prev = m_s[gi]
            gq = bcum[i] + m_prev
            m_t = jnp.maximum(gq, m_in[i])
            sc_in = jnp.exp(m_in[i] - m_t)
            inter = jnp.exp(gq - m_t)
            den = sc_in * d1[i] + inter * qn[gi]
            rden = 1.0 / jnp.maximum(jnp.abs(den), jnp.exp(-m_t))
            o_ref[rows, gi * DV:(gi + 1) * DV] = ((sc_in * rden) * pv[i] + (inter * rden) * qc[gi])
            m_new = jnp.maximum(b_last[i] + m_prev, m_loc[i])
            decay = jnp.exp(b_last[i] + m_prev - m_new)
            sc = jnp.exp(m_loc[i] - m_new)
            c_s[gi] = decay * c_s[gi] + sc * kv[i]
            n_s[gi] = decay * n_s[gi] + sc * ks[i]
            m_s[gi] = m_new


MLSTM_GROUP = 4


def _mlstm_scan(qkvz, g, gt, nct, dr):
    bsz, tt, _ = qkvz.shape
    nt = tt // TM
    reverse = dr == 1
    nh = MLSTM_HEADS
    ng = MLSTM_GROUP
    nchunk = TM // MLSTM_CHUNK
    tile = lambda s: _tile_of_step(s, nt, nct, reverse)
    kern = functools.partial(_mlstm_scan_kernel, dr=dr, reverse=reverse)
    koff = nh // ng
    voff = 2 * nh * MLSTM_DK // (ng * MLSTM_DV)
    ioff = dr * 2 * nh // ng
    foff = (dr * 2 * nh + nh) // ng
    return pl.pallas_call(
        kern,
        grid=(bsz, nh // ng, nt),
        in_specs=[pl.BlockSpec((None, TM, ng * MLSTM_DK), lambda b, h, s: (b, tile(s), h)),
                  pl.BlockSpec((None, TM, ng * MLSTM_DK), lambda b, h, s: (b, tile(s), koff + h)),
                  pl.BlockSpec((None, TM, ng * MLSTM_DV), lambda b, h, s: (b, tile(s), voff + h)),
                  pl.BlockSpec((None, TM, LANES), lambda b, h, s: (b, tile(s), 0)),
                  pl.BlockSpec((None, ng, None, nchunk, MLSTM_CHUNK),
                               lambda b, h, s: (b, ioff + h, tile(s), 0, 0)),
                  pl.BlockSpec((None, ng, None, nchunk, MLSTM_CHUNK),
                               lambda b, h, s: (b, foff + h, tile(s), 0, 0))],
        out_specs=pl.BlockSpec((None, TM, ng * MLSTM_DV), lambda b, h, s: (b, tile(s), h)),
        out_shape=jax.ShapeDtypeStruct((bsz, tt, nh * MLSTM_DV), F32),
        scratch_shapes=[pltpu.VMEM((ng, MLSTM_DK, MLSTM_DV), F32),
                        pltpu.VMEM((ng, 1, MLSTM_DK), F32),
                        pltpu.VMEM((ng, 1, 1), F32)],
        compiler_params=_cp(("arbitrary", "arbitrary", "arbitrary")),
    )(qkvz, qkvz, qkvz, g, gt, gt)


def _mlstm_layer(xs, mods, norm_g, w_in, gate_b, hnorm_g, w_out, nct):
    bsz, tt, d = xs.shape
    nh = MLSTM_HEADS
    qk = nh * MLSTM_DK
    width = nh * MLSTM_DV
    nmain = 2 * qk + 2 * width
    ng = 4 * nh
    g2 = norm_g.reshape(1, d)
    qkvz = _nm_matmul(xs, mods, g2, w_in[:, :nmain].astype(BF16), jnp.zeros((1, nmain), F32),
                      width, nct)
    wg = jnp.pad(w_in[:, nmain:], ((0, 0), (0, LANES - ng))).astype(BF16)
    bg = jnp.pad(gate_b.reshape(1, ng), ((0, 0), (0, LANES - ng)))
    g = _nm_matmul(xs, mods, g2, wg, bg, LANES, nct)
    gt = jnp.transpose(g[:, :, :ng], (0, 2, 1)).reshape(bsz, ng, tt // TM, TM // MLSTM_CHUNK, MLSTM_CHUNK)
    hs = [_mlstm_scan(qkvz, g, gt, nct, dr) for dr in range(2)]
    return pl.pallas_call(
        _mlstm_out_kernel,
        grid=(bsz, tt // TM),
        in_specs=[pl.BlockSpec((None, TM, d), lambda b, i: (b, i, 0)),
                  pl.BlockSpec((None, 3, d), _mod_index(nct)),
                  pl.BlockSpec((None, TM, width), lambda b, i: (b, i, 0)),
                  pl.BlockSpec((None, TM, width), lambda b, i: (b, i, 0)),
                  pl.BlockSpec((None, TM, width), lambda b, i: (b, i, (2 * qk + width) // width)),
                  pl.BlockSpec((1, width), lambda b, i: (0, 0)),
                  pl.BlockSpec((width, d), lambda b, i: (0, 0))],
        out_specs=pl.BlockSpec((None, TM, d), lambda b, i: (b, i, 0)),
        out_shape=jax.ShapeDtypeStruct(xs.shape, F32),
        scratch_shapes=[pltpu.VMEM((TM, width), F32)],
        compiler_params=_cp(("arbitrary", "arbitrary")),
    )(xs, mods, hs[0], hs[1], qkvz, hnorm_g.reshape(1, width), w_out.astype(BF16))


def _rwkv_prep_kernel(x_ref, xp_ref, xn_ref, mod_ref, g_ref, mu_ref, wr_ref, w0_ref, w1_ref, w2_ref,
                      a0_ref, a1_ref, a2_ref, kk_ref, ka_ref, hr_ref, he_ref,
                      r_o, kn_o, v_o, z_o, lw0_o, lw1_o, be0_o, be1_o, k0_o, k1_o,
                      hs, sh, *, nt, nct):
    i = pl.program_id(1)
    seg_first = jnp.logical_or(i == 0, i == nct)
    seg_last = jnp.logical_or(i == nct - 1, i == nt - 1)
    gw = GRID_W
    d = x_ref.shape[1]
    g = g_ref[...]
    hs[0:gw, :] = jnp.where(seg_first, 0.0, _normmod(xp_ref[...], g, mod_ref))
    hs[gw:gw + TM, :] = _normmod(x_ref[...], g, mod_ref)
    hs[gw + TM:2 * gw + TM, :] = jnp.where(seg_last, 0.0, _normmod(xn_ref[...], g, mod_ref))

    @pl.when(i < nct)
    def _():
        half = d // 2
        sh[:, 0:half] = hs[pl.ds(gw - 1, TM), 0:half]
        sh[:, half:d] = hs[pl.ds(gw + 1, TM), half:d]

    @pl.when(i >= nct)
    def _():
        q = d // 4
        wcol = lax.broadcasted_iota(jnp.int32, (TM, q), 0) % gw
        sh[:, 0:q] = jnp.where(wcol == 0, 0.0, hs[pl.ds(gw - 1, TM), 0:q])
        sh[:, q:2 * q] = jnp.where(wcol == gw - 1, 0.0, hs[pl.ds(gw + 1, TM), q:2 * q])
        sh[:, 2 * q:3 * q] = hs[pl.ds(0, TM), 2 * q:3 * q]
        sh[:, 3 * q:d] = hs[pl.ds(2 * gw, TM), 3 * q:d]

    h = hs[pl.ds(gw, TM), :]
    dx = sh[...] - h
    mix = lambda n: h + dx * mu_ref[n:n + 1, :]
    r = _bdot(mix(0), wr_ref[0])
    k = _bdot(mix(1), wr_ref[1])
    r_o[...] = r
    v_o[...] = _bdot(mix(2), wr_ref[2])
    z_o[...] = _bdot(mix(3), wr_ref[3])
    kk = k * kk_ref[...]
    ss = _head_sum(kk * kk, hr_ref, he_ref)
    kn = kk / jnp.maximum(jnp.sqrt(ss), 1e-12)
    kn_o[...] = kn
    xw = mix(4)
    xa = mix(5)
    for dr, (lw_o, be_o, k_o) in enumerate(((lw0_o, be0_o, k0_o), (lw1_o, be1_o, k1_o))):
        w_pre = w0_ref[dr:dr + 1, :] + _bdot(jnp.tanh(_bdot(xw, w1_ref[dr])), w2_ref[dr])
        lw_o[...] = -jnp.exp(-_softplus(-w_pre) - 0.5)
        a = jax.nn.sigmoid(a0_ref[dr:dr + 1, :] + _bdot(_bdot(xa, a1_ref[dr]), a2_ref[dr]))
        be_o[...] = a * kn
        k_o[...] = k * (1.0 + (a - 1.0) * ka_ref[...])


def _chunk_cumsum(x, chunk, reverse):
    n = x.shape[0]
    pos = lax.broadcasted_iota(jnp.int32, x.shape, 0) % chunk
    sh = 1
    while sh < chunk:
        if reverse:
            x = x + jnp.where(pos < chunk - sh, pltpu.roll(x, n - sh, 0), 0.0)
        else:
            x = x + jnp.where(pos >= sh, pltpu.roll(x, sh, 0), 0.0)
        sh *= 2
    return x


def _unit_lower_inverses(ns, strict, eye, blk):
    L = ns[0].shape[0]
    row = lax.broadcasted_iota(jnp.int32, (L, L), 0)
    col = lax.broadcasted_iota(jnp.int32, (L, L), 1)
    same = (row // blk) == (col // blk)
    in_blk = jnp.logical_and(strict, same)
    off_blk = jnp.logical_and(strict, jnp.logical_not(same))
    pws = [jnp.where(in_blk, n, 0.0) for n in ns]
    tds = [eye - p for p in pws]
    for _ in range(blk.bit_length() - 2):
        pws = [_bdot(p, p) for p in pws]
        tds = [_bdot(t, eye + p) for t, p in zip(tds, pws)]
    pws = [_bdot(t, jnp.where(off_blk, n, 0.0)) for t, n in zip(tds, ns)]
    tis = [eye - p for p in pws]
    for _ in range((L // blk).bit_length() - 2):
        pws = [_bdot(p, p) for p in pws]
        tis = [_bdot(t, eye + p) for t, p in zip(tis, pws)]
    return [_bdot(ti, td) for ti, td in zip(tis, tds)]


def _rwkv_scan_kernel(r_ref, kn_ref, v_ref, lw_ref, be_ref, kd_ref, y_ref,
                      st_s, rp_s, y0_s, m_s, z_s, pl_s, *, reverse):
    s = pl.program_id(2)
    L = RWKV_CHUNK
    N = RWKV_HEAD
    nchunk = TM // L
    nhead = r_ref.shape[1] // N

    @pl.when(s == 0)
    def _():
        st_s[...] = jnp.zeros_like(st_s)

    row = lax.broadcasted_iota(jnp.int32, (L, L), 0)
    col = lax.broadcasted_iota(jnp.int32, (L, L), 1)
    incl = (col >= row) if reverse else (col <= row)
    strict = (col > row) if reverse else (col < row)
    eye = (col == row).astype(F32)
    last = 0 if reverse else L - 1

    lw_all = lw_ref[...]
    lc_all = _chunk_cumsum(lw_all, L, reverse)

    a_p, r_p, b_i, k_i, b_l, k_l, vh, where = [], [], [], [], [], [], [], []
    for c in range(nchunk):
        rows = slice(c * L, (c + 1) * L)
        lw = lw_all[rows]
        lc = lc_all[rows]
        lc_last = lc[last:last + 1, :]
        einv = jnp.exp(-lc)
        to_end = jnp.exp(lc_last - lc)
        be = be_ref[rows, :]
        kd = kd_ref[rows, :]
        full = [(kn_ref[rows, :] * jnp.exp(lc - lw)).astype(BF16),
                (r_ref[rows, :] * jnp.exp(lc)).astype(BF16),
                (be * einv).astype(BF16), (kd * einv).astype(BF16),
                (be * to_end).astype(BF16), (kd * to_end).astype(BF16),
                v_ref[rows, :].astype(BF16)]
        pl_s[c] = jnp.exp(lc_last)
        for hh in range(nhead):
            sl = slice(hh * N, (hh + 1) * N)
            for dst, src in zip((a_p, r_p, b_i, k_i, b_l, k_l, vh), full):
                dst.append(src[:, sl])
            where.append((c, rows, hh, sl))
    n_inst = len(where)
    ams = [_dot_nt(jnp.concatenate([a_p[i], r_p[i]], axis=0),
                   jnp.concatenate([b_i[i], k_i[i]], axis=0)) for i in range(n_inst)]
    tinvs = _unit_lower_inverses([am[:L, :L] for am in ams], strict, eye, 16)
    avs = [_bdot(jnp.where(strict, ams[i][:L, L:], 0.0), vh[i]) for i in range(n_inst)]
    xs = [_bdot(tinvs[i], jnp.concatenate([a_p[i].astype(F32), avs[i]], axis=1)) for i in range(n_inst)]
    rks = [_bdot(jnp.where(incl, ams[i][L:, L:], 0.0), vh[i]) for i in range(n_inst)]
    rbs = [_bdot(jnp.where(incl, ams[i][L:, :L], 0.0), xs[i]) for i in range(n_inst)]
    mzs = [_dot_tn(xs[i].astype(BF16), b_l[i]) for i in range(n_inst)]
    vks = [_dot_tn(vh[i], k_l[i]) for i in range(n_inst)]
    for i, (c, rows, hh, sl) in enumerate(where):
        rp_s[rows, sl] = r_p[i].astype(F32) - rbs[i][:, :N]
        y0_s[rows, sl] = rks[i] - rbs[i][:, N:]
        m_s[c, hh] = -mzs[i][:N]
        z_s[c, hh] = vks[i] - mzs[i][N:]

    for j in range(nchunk):
        c = nchunk - 1 - j if reverse else j
        rows = slice(c * L, (c + 1) * L)
        p_l = pl_s[c]
        for hh in range(nhead):
            sl = slice(hh * N, (hh + 1) * N)
            st = st_s[hh]
            st16 = st.astype(BF16)
            y_ref[rows, sl] = _dot_nt(rp_s[rows, sl].astype(BF16), st16) + y0_s[rows, sl]
            st_s[hh] = st * p_l[:, sl] + _bdot(st16, m_s[c, hh]) + z_s[c, hh]


RWKV_GROUP = 4


def _rwkv_scan(r, kn, v, lw, be, kd, nct, reverse):
    bsz, tt, w = r.shape
    nt = tt // TM
    tile = lambda s: _tile_of_step(s, nt, nct, reverse)
    nh = RWKV_GROUP
    gw = nh * RWKV_HEAD
    spec = pl.BlockSpec((None, TM, gw), lambda b, p, s: (b, tile(s), p))
    kern = functools.partial(_rwkv_scan_kernel, reverse=reverse)
    nchunk = TM // RWKV_CHUNK
    return pl.pallas_call(
        kern,
        grid=(bsz, w // gw, nt),
        in_specs=[spec] * 6,
        out_specs=spec,
        out_shape=jax.ShapeDtypeStruct((bsz, tt, w), F32),
        scratch_shapes=[pltpu.VMEM((nh, RWKV_HEAD, RWKV_HEAD), F32),
                        pltpu.VMEM((TM, gw), F32),
                        pltpu.VMEM((TM, gw), F32),
                        pltpu.VMEM((nchunk, nh, RWKV_HEAD, RWKV_HEAD), F32),
                        pltpu.VMEM((nchunk, nh, RWKV_HEAD, RWKV_HEAD), F32),
                        pltpu.VMEM((nchunk, 1, gw), F32)],
        compiler_params=_cp(("arbitrary", "arbitrary", "arbitrary")),
    )(r, kn, v, lw, be, kd)


def _rwkv_layer(xs, mods, norm_g, mu, w_rkvz, w0, w1, w2, a0, a1, a2, k_k, k_a, r_k, ln_g, ln_b,
                w_out, nct):
    bsz, tt, d = xs.shape
    nt = tt // TM
    r64 = TM // GRID_W
    nb64 = tt // GRID_W
    hred = (jnp.arange(d)[:, None] // RWKV_HEAD == jnp.arange(LANES)[None, :]).astype(BF16)
    hexp = hred.T
    row = lambda a: a.reshape(1, d)
    full = lambda a: pl.BlockSpec(a.shape, lambda b, i: (0,) * a.ndim)
    tile_spec = pl.BlockSpec((None, TM, d), lambda b, i: (b, i, 0))
    consts = [row(norm_g), mu, w_rkvz.astype(BF16), w0, w1.astype(BF16), w2.astype(BF16),
              a0, a1.astype(BF16), a2.astype(BF16), row(k_k), row(k_a), hred, hexp]
    kern = functools.partial(_rwkv_prep_kernel, nt=nt, nct=nct)
    outs = pl.pallas_call(
        kern,
        grid=(bsz, nt),
        in_specs=[tile_spec,
                  pl.BlockSpec((None, GRID_W, d), lambda b, i: (b, jnp.maximum(i * r64 - 1, 0), 0)),
                  pl.BlockSpec((None, GRID_W, d), lambda b, i: (b, jnp.minimum((i + 1) * r64, nb64 - 1), 0)),
                  pl.BlockSpec((None, 3, d), _mod_index(nct))] + [full(a) for a in consts],
        out_specs=[tile_spec] * 10,
        out_shape=[jax.ShapeDtypeStruct(xs.shape, F32)] * 10,
        scratch_shapes=[pltpu.VMEM((TM + 2 * GRID_W, d), F32), pltpu.VMEM((TM, d), F32)],
        compiler_params=_cp(("arbitrary", "arbitrary")),
    )(xs, xs, xs, mods, *consts)
    r, kn, v, z, lw0, lw1, be0, be1, k0, k1 = outs
    y0 = _rwkv_scan(r, kn, v, lw0, be0, k0, nct, reverse=False)
    y1 = _rwkv_scan(r, kn, v, lw1, be1, k1, nct, reverse=True)
    fin = [row(r_k), row(ln_g), row(ln_b), hred, hexp, w_out.astype(BF16)]
    return pl.pallas_call(
        _rwkv_out_kernel,
        grid=(bsz, nt),
        in_specs=[tile_spec, pl.BlockSpec((None, 3, d), _mod_index(nct))] + [tile_spec] * 7
                 + [full(a) for a in fin],
        out_specs=tile_spec,
        out_shape=jax.ShapeDtypeStruct(xs.shape, F32),
        compiler_params=_cp(("arbitrary", "arbitrary")),
    )(xs, mods, y0, y1, r, k0, k1, v, z, *fin)


def _final_norm_kernel(x_ref, g_ref, o_ref):
    xv = x_ref[...]
    o_ref[...] = xv * lax.rsqrt(jnp.mean(xv * xv, axis=-1, keepdims=True) + NORM_EPS) * g_ref[...]


def _final_norm(xs, g, nct):
    bsz, tt, d = xs.shape
    t = tt - nct * TM
    return pl.pallas_call(
        _final_norm_kernel,
        grid=(bsz, t // TM),
        in_specs=[pl.BlockSpec((None, TM, d), lambda b, i: (b, i + nct, 0)),
                  pl.BlockSpec((1, d), lambda b, i: (0, 0))],
        out_specs=pl.BlockSpec((None, TM, d), lambda b, i: (b, i, 0)),
        out_shape=jax.ShapeDtypeStruct((bsz, t, d), F32),
        compiler_params=_cp(("arbitrary", "arbitrary")),
    )(xs, g.reshape(1, d))


def kernel(x, c, ctx, c_ctx, norm_g, mod_w, mod_b, final_g, lru_w_in, lru_conv_w, lru_conv_b, lru_gate_w, lru_gate_b, lru_lam, lru_w_out, mlstm_w_in, mlstm_gate_b, mlstm_norm_g, mlstm_w_out, r7_mu, r7_w_rkvz, r7_w0, r7_w1, r7_w2, r7_a0, r7_a1, r7_a2, r7_k_k, r7_k_a, r7_r_k, r7_ln_g, r7_ln_b, r7_w_out):
    bsz, t, d = x.shape
    ctx_len = ctx.shape[1]
    depth = norm_g.shape[0]
    assert ctx_len % TM == 0 and t % TM == 0 and t % GRID_W == 0 and bsz < SUBLANES
    nct = ctx_len // TM

    cc = jnp.zeros((SUBLANES, d), F32).at[:bsz].set(c).at[bsz].set(c_ctx)
    mp = _mod_params(cc, mod_w, mod_b).reshape(depth, SUBLANES, 3, d)
    mods_all = jnp.stack([jnp.broadcast_to(mp[:, bsz][:, None], (depth, bsz, 3, d)), mp[:, :bsz]],
                         axis=2).reshape(depth, 2 * bsz, 3, d)

    xs = jnp.concatenate([ctx, x], axis=1)
    for i in range(depth):
        kind, j = i % 3, i // 3
        mods = mods_all[i]
        if kind == 0:
            xs = _lru_layer(xs, mods, norm_g[i], lru_w_in[j], lru_conv_w[j], lru_conv_b[j],
                            lru_gate_w[j], lru_gate_b[j], lru_lam[j], lru_w_out[j], nct)
        elif kind == 1:
            xs = _mlstm_layer(xs, mods, norm_g[i], mlstm_w_in[j], mlstm_gate_b[j], mlstm_norm_g[j],
                              mlstm_w_out[j], nct)
        else:
            xs = _rwkv_layer(xs, mods, norm_g[i], r7_mu[j], r7_w_rkvz[j], r7_w0[j], r7_w1[j], r7_w2[j],
                             r7_a0[j], r7_a1[j], r7_a2[j], r7_k_k[j], r7_k_a[j], r7_r_k[j],
                             r7_ln_g[j], r7_ln_b[j], r7_w_out[j], nct)
    return _final_norm(xs, final_g, nct)
```

```python
import functools

import jax
import jax.numpy as jnp
from jax import lax
from jax.experimental import pallas as pl
from jax.experimental.pallas import tpu as pltpu

F32 = jnp.float32
BF16 = jnp.bfloat16

GRID_W = 64
LRU_C = 8.0
LRU_BS = 128
CONV_W = 4
MLSTM_HEADS = 8
MLSTM_DK = 128
MLSTM_DV = 256
MLSTM_CHUNK = 64
RWKV_HEAD = 64
RWKV_CHUNK = 64
GN_EPS = 64e-5
NORM_EPS = 1e-6

TM = 256
TB = 512
SUBLANES = 8
LANES = 128
VMEM_LIMIT = 56 * 1024 * 1024


def _cp(sem):
    return pltpu.CompilerParams(dimension_semantics=sem, vmem_limit_bytes=VMEM_LIMIT)


def _silu(v):
    return v * jax.nn.sigmoid(v)


def _softplus(v):
    return jnp.maximum(v, 0.0) + jnp.log1p(jnp.exp(-jnp.abs(v)))


def _log_sigmoid(v):
    return jnp.minimum(v, 0.0) - jnp.log1p(jnp.exp(-jnp.abs(v)))


def _bdot(a, b):
    return jnp.dot(a.astype(BF16), b.astype(BF16), preferred_element_type=F32)


def _dot_nt(a, b):
    return lax.dot_general(a, b, (((1,), (1,)), ((), ())), preferred_element_type=F32)


def _dot_tn(a, b):
    return lax.dot_general(a, b, (((0,), (0,)), ((), ())), preferred_element_type=F32)


def _normmod(xv, g, mod_ref):
    y = xv * lax.rsqrt(jnp.mean(xv * xv, axis=-1, keepdims=True) + NORM_EPS)
    return (y * g) * (1.0 + mod_ref[1:2, :]) + mod_ref[0:1, :]


def _tile_of_step(s, nt, nx, reverse):
    nct = nt - nx
    if not reverse:
        return jnp.where(s < nct, nx + s, s - nct)
    return jnp.where(s < nct, nt - 1 - s, nx - 1 - (s - nct))


def _mod_row(b, i, nxb):
    return b * 2 + jnp.where(i >= nxb, 0, 1)


def _mod_spec(d, nx, rows):
    nxb = nx * TM // rows
    return pl.BlockSpec((None, 3, d), lambda b, i: (_mod_row(b, i, nxb), 0, 0))


def _split3(v):
    hi = v.astype(BF16)
    rest = v - hi.astype(F32)
    mid = rest.astype(BF16)
    lo = (rest - mid.astype(F32)).astype(BF16)
    return hi, mid, lo


def _mod_kernel(c_ref, w_ref, b_ref, o_ref):
    cs = _split3(_silu(c_ref[...]))
    ws = _split3(w_ref[...])
    acc = b_ref[...]
    for i, j in ((2, 0), (1, 1), (0, 2), (1, 0), (0, 1), (0, 0)):
        acc = acc + jnp.dot(cs[i], ws[j], preferred_element_type=F32)
    o_ref[...] = acc


def _mod_params(cc, mod_w, mod_b):
    depth, d, d3 = mod_w.shape
    rows = cc.shape[0]
    return pl.pallas_call(
        _mod_kernel,
        grid=(depth, d3 // d),
        in_specs=[pl.BlockSpec((rows, d), lambda l, j: (0, 0)),
                  pl.BlockSpec((None, d, d), lambda l, j: (l, 0, j)),
                  pl.BlockSpec((None, 1, d), lambda l, j: (l, 0, j))],
        out_specs=pl.BlockSpec((None, rows, d), lambda l, j: (l, 0, j)),
        out_shape=jax.ShapeDtypeStruct((depth, rows, d3), F32),
        compiler_params=_cp(("arbitrary", "arbitrary")),
    )(cc, mod_w, mod_b.reshape(depth, 1, d3))


def _nm_matmul_kernel(x_ref, mod_ref, g_ref, w_ref, b_ref, o_ref):
    h = _normmod(x_ref[...], g_ref[...], mod_ref)
    o_ref[...] = _bdot(h, w_ref[...]) + b_ref[...]


def _nm_matmul(xs, mods, g, w, bias, tn, nx):
    bsz, tt, d = xs.shape
    n = w.shape[1]
    nxb = nx * TM // TB
    return pl.pallas_call(
        _nm_matmul_kernel,
        grid=(n // tn, bsz, pl.cdiv(tt, TB)),
        in_specs=[pl.BlockSpec((None, TB, d), lambda j, b, i: (b, i, 0)),
                  pl.BlockSpec((None, 3, d), lambda j, b, i: (_mod_row(b, i, nxb), 0, 0)),
                  pl.BlockSpec((1, d), lambda j, b, i: (0, 0)),
                  pl.BlockSpec((d, tn), lambda j, b, i: (0, j)),
                  pl.BlockSpec((1, tn), lambda j, b, i: (0, j))],
        out_specs=pl.BlockSpec((None, TB, tn), lambda j, b, i: (b, i, j)),
        out_shape=jax.ShapeDtypeStruct((bsz, tt, n), F32),
        compiler_params=_cp(("arbitrary", "arbitrary", "arbitrary")),
    )(xs, mods, g, w, bias)


def _lru_out_kernel(x_ref, mod_ref, y0_ref, y1_ref, z_ref, w_ref, o_ref):
    y = (y0_ref[...] + y1_ref[...]) * _silu(z_ref[...])
    o_ref[...] = x_ref[...] + mod_ref[2:3, :] * _bdot(y, w_ref[...])


def _mlstm_out_kernel(x_ref, mod_ref, h0_ref, h1_ref, z_ref, ng_ref, w_ref, o_ref, hn_s):
    hs = h0_ref[...] + h1_ref[...]
    for hd in range(MLSTM_HEADS):
        sl = slice(hd * MLSTM_DV, (hd + 1) * MLSTM_DV)
        hh = hs[:, sl]
        hn_s[:, sl] = hh * lax.rsqrt(jnp.mean(hh * hh, axis=-1, keepdims=True) + NORM_EPS)
    y = (hn_s[...] * ng_ref[...]) * _silu(z_ref[...])
    o_ref[...] = x_ref[...] + mod_ref[2:3, :] * _bdot(y, w_ref[...])


def _head_sum(v, hr_ref, he_ref):
    hr = hr_ref[...]
    he = he_ref[...]
    red = sum(jnp.dot(p, hr, preferred_element_type=F32) for p in _split3(v))
    return sum(jnp.dot(p, he, preferred_element_type=F32) for p in _split3(red))


def _rwkv_out_kernel(x_ref, mod_ref, y0_ref, y1_ref, r_ref, k0_ref, k1_ref, v_ref, z_ref,
                     rk_ref, lg_ref, lb_ref, hr_ref, he_ref, w_ref, o_ref):
    inv_n = 1.0 / RWKV_HEAD
    y = y0_ref[...] + y1_ref[...]
    mean = _head_sum(y, hr_ref, he_ref) * inv_n
    yc = y - mean
    var = _head_sum(yc * yc, hr_ref, he_ref) * inv_n
    yn = yc * lax.rsqrt(var + GN_EPS) * lg_ref[...] + lb_ref[...]
    rv = r_ref[...]
    bonus = _head_sum(rv * k0_ref[...] * rk_ref[...] + rv * k1_ref[...] * rk_ref[...], hr_ref, he_ref)
    out = (yn + bonus * v_ref[...]) * _silu(z_ref[...])
    o_ref[...] = x_ref[...] + mod_ref[2:3, :] * _bdot(out, w_ref[...])


def _lru_scan_kernel(u_ref, up_ref, un_ref, cw_ref, cb_ref, gw_ref, gb_ref, lam_ref, y_ref,
                     ext, a_s, b_s, h_s, *, nt, nx, reverse):
    s = pl.program_id(1)
    tile = _tile_of_step(s, nt, nx, reverse)
    seg_first = jnp.logical_or(tile == 0, tile == nx)
    seg_last = jnp.logical_or(tile == nx - 1, tile == nt - 1)
    halo = SUBLANES

    @pl.when(s == 0)
    def _():
        h_s[...] = jnp.zeros_like(h_s)

    ext[0:halo, :] = jnp.where(seg_first, 0.0, up_ref[...])
    ext[halo:halo + TM, :] = u_ref[...]
    ext[halo + TM:2 * halo + TM, :] = jnp.where(seg_last, 0.0, un_ref[...])
    left = CONV_W // 2
    ev = ext[...]
    nrow = ev.shape[0]
    cv = cb_ref[...]
    for j in range(CONV_W):
        tap = ev if j == left else pltpu.roll(ev, (left - j) % nrow, 0)
        cv = cv + tap[halo:halo + TM, :] * cw_ref[j:j + 1, :]

    nblk = cv.shape[1] // LRU_BS
    neg_c_sp = -LRU_C * _softplus(-lam_ref[...])
    for n in range(nblk):
        sl = slice(n * LRU_BS, (n + 1) * LRU_BS)
        blk = cv[:, sl]
        gates = _bdot(blk, gw_ref[n])
        r = jax.nn.sigmoid(gates[:, :LRU_BS] + gb_ref[0:1, sl])
        i = jax.nn.sigmoid(gates[:, LRU_BS:] + gb_ref[1:2, sl])
        log_a = neg_c_sp[:, sl] * r
        a = jnp.exp(log_a)
        a_s[:, sl] = a
        b_s[:, sl] = jnp.sqrt(-jnp.tanh(log_a) * (a * a + 1.0)) * (i * blk)

    def body(j, h):
        t = TM - 1 - j if reverse else j
        h = a_s[pl.ds(t, 1), :] * h + b_s[pl.ds(t, 1), :]
        y_ref[pl.ds(t, 1), :] = h
        return h

    h_s[...] = lax.fori_loop(0, TM, body, h_s[...], unroll=8)


def _lru_scan(uz, conv_w, conv_b, gw, gb, lam, nx, reverse):
    bsz, tt, e2 = uz.shape
    e = e2 // 2
    nt = tt // TM
    r8 = TM // SUBLANES
    nb8 = tt // SUBLANES
    tile = lambda s: _tile_of_step(s, nt, nx, reverse)
    kern = functools.partial(_lru_scan_kernel, nt=nt, nx=nx, reverse=reverse)
    return pl.pallas_call(
        kern,
        grid=(bsz, nt),
        in_specs=[pl.BlockSpec((None, TM, e), lambda b, s: (b, tile(s), 0)),
                  pl.BlockSpec((None, SUBLANES, e), lambda b, s: (b, jnp.maximum(tile(s) * r8 - 1, 0), 0)),
                  pl.BlockSpec((None, SUBLANES, e), lambda b, s: (b, jnp.minimum((tile(s) + 1) * r8, nb8 - 1), 0)),
                  pl.BlockSpec((CONV_W, e), lambda b, s: (0, 0)),
                  pl.BlockSpec((1, e), lambda b, s: (0, 0)),
                  pl.BlockSpec(gw.shape, lambda b, s: (0, 0, 0)),
                  pl.BlockSpec((2, e), lambda b, s: (0, 0)),
                  pl.BlockSpec((1, e), lambda b, s: (0, 0))],
        out_specs=pl.BlockSpec((None, TM, e), lambda b, s: (b, tile(s), 0)),
        out_shape=jax.ShapeDtypeStruct((bsz, tt, e), F32),
        scratch_shapes=[pltpu.VMEM((TM + 2 * SUBLANES, e), F32),
                        pltpu.VMEM((TM, e), F32),
                        pltpu.VMEM((TM, e), F32),
                        pltpu.VMEM((1, e), F32)],
        compiler_params=_cp(("arbitrary", "arbitrary")),
    )(uz, uz, uz, conv_w, conv_b, gw, gb, lam)


def _lru_layer(xs, mods, norm_g, w_in, conv_w, conv_b, gate_w, gate_b, lam, w_out, nx):
    bsz, tt, d = xs.shape
    e = w_out.shape[0]
    nblk = e // LRU_BS
    uz = _nm_matmul(xs, mods, norm_g.reshape(1, d), w_in.astype(BF16),
                    jnp.zeros((1, 2 * e), F32), e, nx)
    ys = []
    for dr in range(2):
        gw = jnp.transpose(gate_w[dr], (1, 2, 0, 3)).reshape(nblk, LRU_BS, 2 * LRU_BS).astype(BF16)
        ys.append(_lru_scan(uz, conv_w, conv_b.reshape(1, e), gw, gate_b[dr], lam[dr].reshape(1, e),
                            nx, reverse=(dr == 1)))
    return pl.pallas_call(
        _lru_out_kernel,
        grid=(bsz, pl.cdiv(tt, TB)),
        in_specs=[pl.BlockSpec((None, TB, d), lambda b, i: (b, i, 0)),
                  _mod_spec(d, nx, TB),
                  pl.BlockSpec((None, TB, e), lambda b, i: (b, i, 0)),
                  pl.BlockSpec((None, TB, e), lambda b, i: (b, i, 0)),
                  pl.BlockSpec((None, TB, e), lambda b, i: (b, i, 1)),
                  pl.BlockSpec((e, d), lambda b, i: (0, 0))],
        out_specs=pl.BlockSpec((None, TB, d), lambda b, i: (b, i, 0)),
        out_shape=jax.ShapeDtypeStruct(xs.shape, F32),
        compiler_params=_cp(("arbitrary", "arbitrary")),
    )(xs, mods, ys[0], ys[1], uz, w_out.astype(BF16))


def _mlstm_scan_kernel(q_ref, k_ref, v_ref, g_ref, it_ref, ft_ref, o_ref, c_s, n_s, m_s,
                       *, dr, reverse):
    hg = pl.program_id(1)
    s = pl.program_id(2)
    L = MLSTM_CHUNK
    DK, DV = MLSTM_DK, MLSTM_DV
    nchunk = TM // L
    ng = q_ref.shape[1] // DK

    @pl.when(s == 0)
    def _():
        c_s[...] = jnp.zeros_like(c_s)
        n_s[...] = jnp.zeros_like(n_s)
        m_s[...] = jnp.zeros_like(m_s)

    row = lax.broadcasted_iota(jnp.int32, (L, L), 0)
    col = lax.broadcasted_iota(jnp.int32, (L, L), 1)
    causal = (col >= row) if reverse else (col <= row)
    causal_t = (row >= col) if reverse else (row <= col)
    lane = lax.broadcasted_iota(jnp.int32, (TM, LANES), 1)
    last = 0 if reverse else L - 1
    g = g_ref[...]

    inst = []
    q16, bcum, m_in, pv, d1, b_last, m_loc, kv, ks, qf = [], [], [], [], [], [], [], [], [], []
    for gi in range(ng):
        i_lane = dr * 2 * MLSTM_HEADS + hg * ng + gi
        i_col_t = jnp.sum(jnp.where(lane == i_lane, g, 0.0), axis=1, keepdims=True)
        f_col_t = _log_sigmoid(jnp.sum(jnp.where(lane == i_lane + MLSTM_HEADS, g, 0.0),
                                       axis=1, keepdims=True))
        for c in range(nchunk):
            rows = slice(c * L, (c + 1) * L)
            i_row = it_ref[gi, c:c + 1, :]
            f_row = _log_sigmoid(ft_ref[gi, c:c + 1, :])
            bc = jnp.sum(jnp.where(causal, f_row, 0.0), axis=1, keepdims=True)
            bcum_row = jnp.sum(jnp.where(causal_t, f_col_t[rows], 0.0), axis=0, keepdims=True)
            logw = jnp.where(causal, bc - bcum_row + i_row, -jnp.inf)
            mi = jnp.max(logw, axis=1, keepdims=True)
            q = q_ref[rows, gi * DK:(gi + 1) * DK]
            k = k_ref[rows, gi * DK:(gi + 1) * DK] * (DK ** -0.5)
            v = v_ref[rows, gi * DV:(gi + 1) * DV].astype(BF16)
            sm = _dot_nt(q.astype(BF16), k.astype(BF16)) * jnp.exp(logw - mi)
            bl = bc[last:last + 1, :]
            tot = bl - bc + i_col_t[rows]
            ml = jnp.max(tot, axis=0, keepdims=True)
            kw = k * jnp.exp(tot - ml)
            inst.append((c, gi))
            qf.append(q)
            q16.append(q.astype(BF16))
            bcum.append(bc)
            m_in.append(mi)
            pv.append(_bdot(sm, v))
            d1.append(jnp.sum(sm, axis=1, keepdims=True))
            b_last.append(bl)
            m_loc.append(ml)
            kv.append(_dot_tn(kw.astype(BF16), v))
            ks.append(jnp.sum(kw, axis=0, keepdims=True))

    for j in range(nchunk):
        c = nchunk - 1 - j if reverse else j
        rows = slice(c * L, (c + 1) * L)
        ids = [inst.index((c, gi)) for gi in range(ng)]
        qc = [_bdot(q16[i], c_s[gi]) for gi, i in enumerate(ids)]
        qn = [jnp.sum(qf[i] * n_s[gi], axis=1, keepdims=True) for gi, i in enumerate(ids)]
        for gi, i in enumerate(ids):
            m_prev = m_s[gi]
            gq = bcum[i] + m_prev
            m_t = jnp.maximum(gq, m_in[i])
            sc_in = jnp.exp(m_in[i] - m_t)
            inter = jnp.exp(gq - m_t)
            den = sc_in * d1[i] + inter * qn[gi]
            rden = 1.0 / jnp.maximum(jnp.abs(den), jnp.exp(-m_t))
            o_ref[rows, gi * DV:(gi + 1) * DV] = ((sc_in * rden) * pv[i] + (inter * rden) * qc[gi])
            m_new = jnp.maximum(b_last[i] + m_prev, m_loc[i])
            decay = jnp.exp(b_last[i] + m_prev - m_new)
            sc = jnp.exp(m_loc[i] - m_new)
            c_s[gi] = decay * c_s[gi] + sc * kv[i]
            n_s[gi] = decay * n_s[gi] + sc * ks[i]
            m_s[gi] = m_new


MLSTM_GROUP = 4


def _mlstm_scan(qkvz, g, gt, nx, dr):
    bsz, tt, _ = qkvz.shape
    nt = tt // TM
    reverse = dr == 1
    nh = MLSTM_HEADS
    ng = MLSTM_GROUP
    nchunk = TM // MLSTM_CHUNK
    tile = lambda s: _tile_of_step(s, nt, nx, reverse)
    kern = functools.partial(_mlstm_scan_kernel, dr=dr, reverse=reverse)
    koff = nh // ng
    voff = 2 * nh * MLSTM_DK // (ng * MLSTM_DV)
    ioff = dr * 2 * nh // ng
    foff = (dr * 2 * nh + nh) // ng
    return pl.pallas_call(
        kern,
        grid=(bsz, nh // ng, nt),
        in_specs=[pl.BlockSpec((None, TM, ng * MLSTM_DK), lambda b, h, s: (b, tile(s), h)),
                  pl.BlockSpec((None, TM, ng * MLSTM_DK), lambda b, h, s: (b, tile(s), koff + h)),
                  pl.BlockSpec((None, TM, ng * MLSTM_DV), lambda b, h, s: (b, tile(s), voff + h)),
                  pl.BlockSpec((None, TM, LANES), lambda b, h, s: (b, tile(s), 0)),
                  pl.BlockSpec((None, ng, None, nchunk, MLSTM_CHUNK),
                               lambda b, h, s: (b, ioff + h, tile(s), 0, 0)),
                  pl.BlockSpec((None, ng, None, nchunk, MLSTM_CHUNK),
                               lambda b, h, s: (b, foff + h, tile(s), 0, 0))],
        out_specs=pl.BlockSpec((None, TM, ng * MLSTM_DV), lambda b, h, s: (b, tile(s), h)),
        out_shape=jax.ShapeDtypeStruct((bsz, tt, nh * MLSTM_DV), F32),
        scratch_shapes=[pltpu.VMEM((ng, MLSTM_DK, MLSTM_DV), F32),
                        pltpu.VMEM((ng, 1, MLSTM_DK), F32),
                        pltpu.VMEM((ng, 1, 1), F32)],
        compiler_params=_cp(("arbitrary", "arbitrary", "arbitrary")),
    )(qkvz, qkvz, qkvz, g, gt, gt)


def _mlstm_layer(xs, mods, norm_g, w_in, gate_b, hnorm_g, w_out, nx):
    bsz, tt, d = xs.shape
    nh = MLSTM_HEADS
    qk = nh * MLSTM_DK
    width = nh * MLSTM_DV
    nmain = 2 * qk + 2 * width
    ng = 4 * nh
    g2 = norm_g.reshape(1, d)
    qkvz = _nm_matmul(xs, mods, g2, w_in[:, :nmain].astype(BF16), jnp.zeros((1, nmain), F32),
                      width, nx)
    wg = jnp.pad(w_in[:, nmain:], ((0, 0), (0, LANES - ng))).astype(BF16)
    bg = jnp.pad(gate_b.reshape(1, ng), ((0, 0), (0, LANES - ng)))
    g = _nm_matmul(xs, mods, g2, wg, bg, LANES, nx)
    gt = jnp.transpose(g[:, :, :ng], (0, 2, 1)).reshape(bsz, ng, tt // TM, TM // MLSTM_CHUNK, MLSTM_CHUNK)
    hs = [_mlstm_scan(qkvz, g, gt, nx, dr) for dr in range(2)]
    return pl.pallas_call(
        _mlstm_out_kernel,
        grid=(bsz, pl.cdiv(tt, TB)),
        in_specs=[pl.BlockSpec((None, TB, d), lambda b, i: (b, i, 0)),
                  _mod_spec(d, nx, TB),
                  pl.BlockSpec((None, TB, width), lambda b, i: (b, i, 0)),
                  pl.BlockSpec((None, TB, width), lambda b, i: (b, i, 0)),
                  pl.BlockSpec((None, TB, width), lambda b, i: (b, i, (2 * qk + width) // width)),
                  pl.BlockSpec((1, width), lambda b, i: (0, 0)),
                  pl.BlockSpec((width, d), lambda b, i: (0, 0))],
        out_specs=pl.BlockSpec((None, TB, d), lambda b, i: (b, i, 0)),
        out_shape=jax.ShapeDtypeStruct(xs.shape, F32),
        scratch_shapes=[pltpu.VMEM((TB, width), F32)],
        compiler_params=_cp(("arbitrary", "arbitrary")),
    )(xs, mods, hs[0], hs[1], qkvz, hnorm_g.reshape(1, width), w_out.astype(BF16))


def _rwkv_prep_kernel(x_ref, xp_ref, xn_ref, mod_ref, g_ref, mu_ref, wr_ref, w0_ref, w1_ref, w2_ref,
                      a0_ref, a1_ref, a2_ref, kk_ref, ka_ref, hr_ref, he_ref,
                      r_o, kn_o, v_o, z_o, lw0_o, lw1_o, be0_o, be1_o, k0_o, k1_o,
                      hs, sh, *, nt, nx):
    i = pl.program_id(1)
    seg_first = jnp.logical_or(i == 0, i == nx)
    seg_last = jnp.logical_or(i == nx - 1, i == nt - 1)
    gw = GRID_W
    d = x_ref.shape[1]
    g = g_ref[...]
    hs[0:gw, :] = jnp.where(seg_first, 0.0, _normmod(xp_ref[...], g, mod_ref))
    hs[gw:gw + TM, :] = _normmod(x_ref[...], g, mod_ref)
    hs[gw + TM:2 * gw + TM, :] = jnp.where(seg_last, 0.0, _normmod(xn_ref[...], g, mod_ref))

    @pl.when(i >= nx)
    def _():
        half = d // 2
        sh[:, 0:half] = hs[pl.ds(gw - 1, TM), 0:half]
        sh[:, half:d] = hs[pl.ds(gw + 1, TM), half:d]

    @pl.when(i < nx)
    def _():
        q = d // 4
        wcol = lax.broadcasted_iota(jnp.int32, (TM, q), 0) % gw
        sh[:, 0:q] = jnp.where(wcol == 0, 0.0, hs[pl.ds(gw - 1, TM), 0:q])
        sh[:, q:2 * q] = jnp.where(wcol == gw - 1, 0.0, hs[pl.ds(gw + 1, TM), q:2 * q])
        sh[:, 2 * q:3 * q] = hs[pl.ds(0, TM), 2 * q:3 * q]
        sh[:, 3 * q:d] = hs[pl.ds(2 * gw, TM), 3 * q:d]

    h = hs[pl.ds(gw, TM), :]
    dx = sh[...] - h
    mix = lambda n: h + dx * mu_ref[n:n + 1, :]
    r = _bdot(mix(0), wr_ref[0])
    k = _bdot(mix(1), wr_ref[1])
    r_o[...] = r
    v_o[...] = _bdot(mix(2), wr_ref[2])
    z_o[...] = _bdot(mix(3), wr_ref[3])
    kk = k * kk_ref[...]
    ss = _head_sum(kk * kk, hr_ref, he_ref)
    kn = kk / jnp.maximum(jnp.sqrt(ss), 1e-12)
    kn_o[...] = kn
    xw = mix(4)
    xa = mix(5)
    for dr, (lw_o, be_o, k_o) in enumerate(((lw0_o, be0_o, k0_o), (lw1_o, be1_o, k1_o))):
        w_pre = w0_ref[dr:dr + 1, :] + _bdot(jnp.tanh(_bdot(xw, w1_ref[dr])), w2_ref[dr])
        lw_o[...] = -jnp.exp(-_softplus(-w_pre) - 0.5)
        a = jax.nn.sigmoid(a0_ref[dr:dr + 1, :] + _bdot(_bdot(xa, a1_ref[dr]), a2_ref[dr]))
        be_o[...] = a * kn
        k_o[...] = k * (1.0 + (a - 1.0) * ka_ref[...])


def _chunk_cumsum(x, chunk, reverse):
    n = x.shape[0]
    pos = lax.broadcasted_iota(jnp.int32, x.shape, 0) % chunk
    sh = 1
    while sh < chunk:
        if reverse:
            x = x + jnp.where(pos < chunk - sh, pltpu.roll(x, n - sh, 0), 0.0)
        else:
            x = x + jnp.where(pos >= sh, pltpu.roll(x, sh, 0), 0.0)
        sh *= 2
    return x


def _bd(p):
    lane = lax.broadcasted_iota(jnp.int32, p.shape, 1)
    zero = jnp.zeros_like(p)
    return jnp.concatenate([jnp.where(lane < RWKV_HEAD, p, zero),
                            jnp.where(lane >= RWKV_HEAD, p, zero)], axis=0)


def _pdot(a, b):
    return jnp.dot(a.astype(BF16), _bd(b.astype(BF16)), preferred_element_type=F32)


def _pdot_tn(a, b):
    full = _dot_tn(a.astype(BF16), b.astype(BF16))
    lane = lax.broadcasted_iota(jnp.int32, (RWKV_HEAD, 2 * RWKV_HEAD), 1)
    return jnp.where(lane < RWKV_HEAD, full[:RWKV_HEAD], full[RWKV_HEAD:])


def _unit_lower_inverses(ns, strict, eye, blk):
    L = ns[0].shape[0]
    row = lax.broadcasted_iota(jnp.int32, ns[0].shape, 0)
    col = lax.broadcasted_iota(jnp.int32, ns[0].shape, 1) % L
    same = (row // blk) == (col // blk)
    in_blk = jnp.logical_and(strict, same)
    off_blk = jnp.logical_and(strict, jnp.logical_not(same))
    pws = [jnp.where(in_blk, n, 0.0) for n in ns]
    tds = [eye - p for p in pws]
    for _ in range(blk.bit_length() - 2):
        pws = [_pdot(p, p) for p in pws]
        tds = [_pdot(t, eye + p) for t, p in zip(tds, pws)]
    pws = [_pdot(t, jnp.where(off_blk, n, 0.0)) for t, n in zip(tds, ns)]
    tis = [eye - p for p in pws]
    for _ in range((L // blk).bit_length() - 2):
        pws = [_pdot(p, p) for p in pws]
        tis = [_pdot(t, eye + p) for t, p in zip(tis, pws)]
    return [_pdot(ti, td) for ti, td in zip(tis, tds)]


def _rwkv_scan_kernel(r_ref, kn_ref, v_ref, lw_ref, be_ref, kd_ref, y_ref,
                      st_s, rp_s, y0_s, m_s, z_s, pl_s, *, reverse):
    s = pl.program_id(2)
    L = RWKV_CHUNK
    N = RWKV_HEAD
    PW = 2 * N
    assert L == N
    nchunk = TM // L
    npair = r_ref.shape[1] // PW

    @pl.when(s == 0)
    def _():
        st_s[...] = jnp.zeros_like(st_s)

    row = lax.broadcasted_iota(jnp.int32, (L, PW), 0)
    col = lax.broadcasted_iota(jnp.int32, (L, PW), 1) % L
    incl = (col >= row) if reverse else (col <= row)
    strict = (col > row) if reverse else (col < row)
    eye = (col == row).astype(F32)
    last = 0 if reverse else L - 1

    lw_all = lw_ref[...]
    lc_all = _chunk_cumsum(lw_all, L, reverse)

    a_p, r_p, b_i, k_i, b_l, k_l, vh, where = [], [], [], [], [], [], [], []
    for c in range(nchunk):
        rows = slice(c * L, (c + 1) * L)
        lw = lw_all[rows]
        lc = lc_all[rows]
        lc_last = lc[last:last + 1, :]
        einv = jnp.exp(-lc)
        to_end = jnp.exp(lc_last - lc)
        be = be_ref[rows, :]
        kd = kd_ref[rows, :]
        full = [(kn_ref[rows, :] * jnp.exp(lc - lw)).astype(BF16),
                (r_ref[rows, :] * jnp.exp(lc)).astype(BF16),
                (be * einv).astype(BF16), (kd * einv).astype(BF16),
                (be * to_end).astype(BF16), (kd * to_end).astype(BF16),
                v_ref[rows, :].astype(BF16)]
        pl_s[c] = jnp.exp(lc_last)
        for pp in range(npair):
            sl = slice(pp * PW, (pp + 1) * PW)
            for dst, src in zip((a_p, r_p, b_i, k_i, b_l, k_l, vh), full):
                dst.append(src[:, sl])
            where.append((c, rows, pp, sl))
    n_inst = len(where)
    ams = [_dot_nt(jnp.concatenate([a_p[i], r_p[i]], axis=0),
                   jnp.concatenate([_bd(b_i[i]), _bd(k_i[i])], axis=0)) for i in range(n_inst)]
    tinvs = _unit_lower_inverses([am[:L, :PW] for am in ams], strict, eye, 16)
    avs = [_pdot(jnp.where(strict, ams[i][:L, PW:], 0.0), vh[i]) for i in range(n_inst)]
    xa = [_pdot(tinvs[i], a_p[i]) for i in range(n_inst)]
    xw = [_pdot(tinvs[i], avs[i]) for i in range(n_inst)]
    rks = [_pdot(jnp.where(incl, ams[i][L:, PW:], 0.0), vh[i]) for i in range(n_inst)]
    rbs = [jnp.where(incl, ams[i][L:, :PW], 0.0) for i in range(n_inst)]
    rpv = [r_p[i].astype(F32) - _pdot(rbs[i], xa[i]) for i in range(n_inst)]
    y0v = [rks[i] - _pdot(rbs[i], xw[i]) for i in range(n_inst)]
    mv = [_pdot_tn(xa[i], b_l[i]) for i in range(n_inst)]
    zv = [_pdot_tn(vh[i], k_l[i]) - _pdot_tn(xw[i], b_l[i]) for i in range(n_inst)]
    for i, (c, rows, pp, sl) in enumerate(where):
        rp_s[rows, sl] = rpv[i]
        y0_s[rows, sl] = y0v[i]
        m_s[c, pp] = -mv[i]
        z_s[c, pp] = zv[i]

    for j in range(nchunk):
        c = nchunk - 1 - j if reverse else j
        rows = slice(c * L, (c + 1) * L)
        p_l = pl_s[c]
        for pp in range(npair):
            sl = slice(pp * PW, (pp + 1) * PW)
            st = st_s[pp]
            y_ref[rows, sl] = _dot_nt(rp_s[rows, sl].astype(BF16), _bd(st.astype(BF16))) + y0_s[rows, sl]
            st_s[pp] = st * p_l[:, sl] + _pdot(st, m_s[c, pp]) + z_s[c, pp]


RWKV_GROUP = 8


def _rwkv_scan(r, kn, v, lw, be, kd, nx, reverse):
    bsz, tt, w = r.shape
    nt = tt // TM
    tile = lambda s: _tile_of_step(s, nt, nx, reverse)
    nh = RWKV_GROUP
    gw = nh * RWKV_HEAD
    spec = pl.BlockSpec((None, TM, gw), lambda b, p, s: (b, tile(s), p))
    kern = functools.partial(_rwkv_scan_kernel, reverse=reverse)
    nchunk = TM // RWKV_CHUNK
    return pl.pallas_call(
        kern,
        grid=(bsz, w // gw, nt),
        in_specs=[spec] * 6,
        out_specs=spec,
        out_shape=jax.ShapeDtypeStruct((bsz, tt, w), F32),
        scratch_shapes=[pltpu.VMEM((nh // 2, RWKV_HEAD, 2 * RWKV_HEAD), F32),
                        pltpu.VMEM((TM, gw), F32),
                        pltpu.VMEM((TM, gw), F32),
                        pltpu.VMEM((nchunk, nh // 2, RWKV_HEAD, 2 * RWKV_HEAD), F32),
                        pltpu.VMEM((nchunk, nh // 2, RWKV_HEAD, 2 * RWKV_HEAD), F32),
                        pltpu.VMEM((nchunk, 1, gw), F32)],
        compiler_params=_cp(("arbitrary", "arbitrary", "arbitrary")),
    )(r, kn, v, lw, be, kd)


def _rwkv_layer(xs, mods, norm_g, mu, w_rkvz, w0, w1, w2, a0, a1, a2, k_k, k_a, r_k, ln_g, ln_b,
                w_out, nx):
    bsz, tt, d = xs.shape
    nt = tt // TM
    r64 = TM // GRID_W
    nb64 = tt // GRID_W
    hred = (jnp.arange(d)[:, None] // RWKV_HEAD == jnp.arange(LANES)[None, :]).astype(BF16)
    hexp = hred.T
    row = lambda a: a.reshape(1, d)
    full = lambda a: pl.BlockSpec(a.shape, lambda b, i: (0,) * a.ndim)
    tile_spec = pl.BlockSpec((None, TM, d), lambda b, i: (b, i, 0))
    big_spec = pl.BlockSpec((None, TB, d), lambda b, i: (b, i, 0))
    consts = [row(norm_g), mu, w_rkvz.astype(BF16), w0, w1.astype(BF16), w2.astype(BF16),
              a0, a1.astype(BF16), a2.astype(BF16), row(k_k), row(k_a), hred, hexp]
    kern = functools.partial(_rwkv_prep_kernel, nt=nt, nx=nx)
    outs = pl.pallas_call(
        kern,
        grid=(bsz, nt),
        in_specs=[tile_spec,
                  pl.BlockSpec((None, GRID_W, d), lambda b, i: (b, jnp.maximum(i * r64 - 1, 0), 0)),
                  pl.BlockSpec((None, GRID_W, d), lambda b, i: (b, jnp.minimum((i + 1) * r64, nb64 - 1), 0)),
                  _mod_spec(d, nx, TM)] + [full(a) for a in consts],
        out_specs=[tile_spec] * 10,
        out_shape=[jax.ShapeDtypeStruct(xs.shape, F32)] * 10,
        scratch_shapes=[pltpu.VMEM((TM + 2 * GRID_W, d), F32), pltpu.VMEM((TM, d), F32)],
        compiler_params=_cp(("arbitrary", "arbitrary")),
    )(xs, xs, xs, mods, *consts)
    r, kn, v, z, lw0, lw1, be0, be1, k0, k1 = outs
    y0 = _rwkv_scan(r, kn, v, lw0, be0, k0, nx, reverse=False)
    y1 = _rwkv_scan(r, kn, v, lw1, be1, k1, nx, reverse=True)
    fin = [row(r_k), row(ln_g), row(ln_b), hred, hexp, w_out.astype(BF16)]
    return pl.pallas_call(
        _rwkv_out_kernel,
        grid=(bsz, pl.cdiv(tt, TB)),
        in_specs=[big_spec, _mod_spec(d, nx, TB)] + [big_spec] * 7 + [full(a) for a in fin],
        out_specs=big_spec,
        out_shape=jax.ShapeDtypeStruct(xs.shape, F32),
        compiler_params=_cp(("arbitrary", "arbitrary")),
    )(xs, mods, y0, y1, r, k0, k1, v, z, *fin)


def _final_norm_kernel(x_ref, g_ref, o_ref):
    xv = x_ref[...]
    o_ref[...] = xv * lax.rsqrt(jnp.mean(xv * xv, axis=-1, keepdims=True) + NORM_EPS) * g_ref[...]


def _final_norm(xs, g, t):
    bsz, _, d = xs.shape
    return pl.pallas_call(
        _final_norm_kernel,
        grid=(bsz, t // TB),
        in_specs=[pl.BlockSpec((None, TB, d), lambda b, i: (b, i, 0)),
                  pl.BlockSpec((1, d), lambda b, i: (0, 0))],
        out_specs=pl.BlockSpec((None, TB, d), lambda b, i: (b, i, 0)),
        out_shape=jax.ShapeDtypeStruct((bsz, t, d), F32),
        compiler_params=_cp(("arbitrary", "arbitrary")),
    )(xs, g.reshape(1, d))


def kernel(x, c, ctx, c_ctx, norm_g, mod_w, mod_b, final_g, lru_w_in, lru_conv_w, lru_conv_b, lru_gate_w, lru_gate_b, lru_lam, lru_w_out, mlstm_w_in, mlstm_gate_b, mlstm_norm_g, mlstm_w_out, r7_mu, r7_w_rkvz, r7_w0, r7_w1, r7_w2, r7_a0, r7_a1, r7_a2, r7_k_k, r7_k_a, r7_r_k, r7_ln_g, r7_ln_b, r7_w_out):
    bsz, t, d = x.shape
    ctx_len = ctx.shape[1]
    depth = norm_g.shape[0]
    assert ctx_len % TM == 0 and t % TB == 0 and TB % TM == 0 and t % GRID_W == 0 and bsz < SUBLANES
    nx = t // TM

    cc = jnp.zeros((SUBLANES, d), F32).at[:bsz].set(c).at[bsz].set(c_ctx)
    mp = _mod_params(cc, mod_w, mod_b).reshape(depth, SUBLANES, 3, d)
    mods_all = jnp.stack([jnp.broadcast_to(mp[:, bsz][:, None], (depth, bsz, 3, d)), mp[:, :bsz]],
                         axis=2).reshape(depth, 2 * bsz, 3, d)

    xs = jnp.concatenate([x, ctx], axis=1)
    for i in range(depth):
        kind, j = i % 3, i // 3
        mods = mods_all[i]
        if kind == 0:
            xs = _lru_layer(xs, mods, norm_g[i], lru_w_in[j], lru_conv_w[j], lru_conv_b[j],
                            lru_gate_w[j], lru_gate_b[j], lru_lam[j], lru_w_out[j], nx)
        elif kind == 1:
            xs = _mlstm_layer(xs, mods, norm_g[i], mlstm_w_in[j], mlstm_gate_b[j], mlstm_norm_g[j],
                              mlstm_w_out[j], nx)
        else:
            xs = _rwkv_layer(xs, mods, norm_g[i], r7_mu[j], r7_w_rkvz[j], r7_w0[j], r7_w1[j], r7_w2[j],
                             r7_a0[j], r7_a1[j], r7_a2[j], r7_k_k[j], r7_k_a[j], r7_r_k[j],
                             r7_ln_g[j], r7_ln_b[j], r7_w_out[j], nx)
    return _final_norm(xs, final_g, t)
```

```python
import functools

import jax
import jax.numpy as jnp
from jax import lax
from jax.experimental import pallas as pl
from jax.experimental.pallas import tpu as pltpu

F32 = jnp.float32
BF16 = jnp.bfloat16

GRID_W = 64
LRU_C = 8.0
LRU_BS = 128
CONV_W = 4
MLSTM_HEADS = 8
MLSTM_DK = 128
MLSTM_DV = 256
MLSTM_CHUNK = 64
RWKV_HEAD = 64
RWKV_CHUNK = 64
GN_EPS = 64e-5
NORM_EPS = 1e-6

TM = 256
TB = 512
SUBLANES = 8
LANES = 128
VMEM_LIMIT = 56 * 1024 * 1024


def _cp(sem):
    return pltpu.CompilerParams(dimension_semantics=sem, vmem_limit_bytes=VMEM_LIMIT)


def _sigmoid(v):
    return 0.5 * jnp.tanh(0.5 * v) + 0.5


def _silu(v):
    return v * _sigmoid(v)


def _softplus(v):
    return jnp.maximum(v, 0.0) + jnp.log1p(jnp.exp(-jnp.abs(v)))


def _log_sigmoid(v):
    return jnp.minimum(v, 0.0) - jnp.log1p(jnp.exp(-jnp.abs(v)))


def _bdot(a, b):
    return jnp.dot(a.astype(BF16), b.astype(BF16), preferred_element_type=F32)


def _dot_nt(a, b):
    return lax.dot_general(a, b, (((1,), (1,)), ((), ())), preferred_element_type=F32)


def _dot_tn(a, b):
    return lax.dot_general(a, b, (((0,), (0,)), ((), ())), preferred_element_type=F32)


def _normmod(xv, g, mod_ref):
    y = xv * lax.rsqrt(jnp.mean(xv * xv, axis=-1, keepdims=True) + NORM_EPS)
    return (y * g) * (1.0 + mod_ref[1:2, :]) + mod_ref[0:1, :]


def _tile_of_step(s, nt, nx, reverse):
    nct = nt - nx
    if not reverse:
        return jnp.where(s < nct, nx + s, s - nct)
    return jnp.where(s < nct, nt - 1 - s, nx - 1 - (s - nct))


def _mod_row(b, i, nxb):
    return b * 2 + jnp.where(i >= nxb, 0, 1)


def _mod_spec(d, nx, rows):
    nxb = nx * TM // rows
    return pl.BlockSpec((None, 3, d), lambda b, i: (_mod_row(b, i, nxb), 0, 0))


def _split3(v):
    hi = v.astype(BF16)
    rest = v - hi.astype(F32)
    mid = rest.astype(BF16)
    lo = (rest - mid.astype(F32)).astype(BF16)
    return hi, mid, lo


def _mod_kernel(c_ref, w_ref, b_ref, o_ref):
    cs = _split3(_silu(c_ref[...]))
    ws = _split3(w_ref[...])
    acc = b_ref[...]
    for i, j in ((2, 0), (1, 1), (0, 2), (1, 0), (0, 1), (0, 0)):
        acc = acc + jnp.dot(cs[i], ws[j], preferred_element_type=F32)
    o_ref[...] = acc


def _mod_params(cc, mod_w, mod_b):
    depth, d, d3 = mod_w.shape
    rows = cc.shape[0]
    return pl.pallas_call(
        _mod_kernel,
        grid=(depth, d3 // d),
        in_specs=[pl.BlockSpec((rows, d), lambda l, j: (0, 0)),
                  pl.BlockSpec((None, d, d), lambda l, j: (l, 0, j)),
                  pl.BlockSpec((None, 1, d), lambda l, j: (l, 0, j))],
        out_specs=pl.BlockSpec((None, rows, d), lambda l, j: (l, 0, j)),
        out_shape=jax.ShapeDtypeStruct((depth, rows, d3), F32),
        compiler_params=_cp(("arbitrary", "arbitrary")),
    )(cc, mod_w, mod_b.reshape(depth, 1, d3))


def _nm_matmul_kernel(x_ref, mod_ref, g_ref, w_ref, b_ref, o_ref):
    h = _normmod(x_ref[...], g_ref[...], mod_ref)
    o_ref[...] = _bdot(h, w_ref[...]) + b_ref[...]


def _nm_matmul(xs, mods, g, w, bias, tn, nx):
    bsz, tt, d = xs.shape
    n = w.shape[1]
    nxb = nx * TM // TB
    return pl.pallas_call(
        _nm_matmul_kernel,
        grid=(n // tn, bsz, pl.cdiv(tt, TB)),
        in_specs=[pl.BlockSpec((None, TB, d), lambda j, b, i: (b, i, 0)),
                  pl.BlockSpec((None, 3, d), lambda j, b, i: (_mod_row(b, i, nxb), 0, 0)),
                  pl.BlockSpec((1, d), lambda j, b, i: (0, 0)),
                  pl.BlockSpec((d, tn), lambda j, b, i: (0, j)),
                  pl.BlockSpec((1, tn), lambda j, b, i: (0, j))],
        out_specs=pl.BlockSpec((None, TB, tn), lambda j, b, i: (b, i, j)),
        out_shape=jax.ShapeDtypeStruct((bsz, tt, n), F32),
        compiler_params=_cp(("arbitrary", "arbitrary", "arbitrary")),
    )(xs, mods, g, w, bias)


def _lru_out_kernel(x_ref, mod_ref, y0_ref, y1_ref, z_ref, w_ref, o_ref):
    y = (y0_ref[...] + y1_ref[...]) * _silu(z_ref[...])
    o_ref[...] = x_ref[...] + mod_ref[2:3, :] * _bdot(y, w_ref[...])


def _lru_out_final_kernel(x_ref, mod_ref, y0_ref, y1_ref, z_ref, w_ref, fg_ref, o_ref):
    y = (y0_ref[...] + y1_ref[...]) * _silu(z_ref[...])
    xv = x_ref[...] + mod_ref[2:3, :] * _bdot(y, w_ref[...])
    o_ref[...] = xv * lax.rsqrt(jnp.mean(xv * xv, axis=-1, keepdims=True) + NORM_EPS) * fg_ref[...]


def _mlstm_out_kernel(x_ref, mod_ref, h0_ref, h1_ref, z_ref, ng_ref, w_ref, o_ref, hn_s):
    hs = h0_ref[...] + h1_ref[...]
    for hd in range(MLSTM_HEADS):
        sl = slice(hd * MLSTM_DV, (hd + 1) * MLSTM_DV)
        hh = hs[:, sl]
        hn_s[:, sl] = hh * lax.rsqrt(jnp.mean(hh * hh, axis=-1, keepdims=True) + NORM_EPS)
    y = (hn_s[...] * ng_ref[...]) * _silu(z_ref[...])
    o_ref[...] = x_ref[...] + mod_ref[2:3, :] * _bdot(y, w_ref[...])


def _head_sum(v, hr_ref, he_ref):
    hr = hr_ref[...]
    he = he_ref[...]
    red = sum(jnp.dot(p, hr, preferred_element_type=F32) for p in _split3(v))
    return sum(jnp.dot(p, he, preferred_element_type=F32) for p in _split3(red))


def _rwkv_out_kernel(x_ref, mod_ref, y0_ref, y1_ref, r_ref, k0_ref, k1_ref, v_ref, z_ref,
                     rk_ref, lg_ref, lb_ref, hr_ref, he_ref, w_ref, o_ref):
    inv_n = 1.0 / RWKV_HEAD
    y = y0_ref[...] + y1_ref[...]
    mean = _head_sum(y, hr_ref, he_ref) * inv_n
    yc = y - mean
    var = _head_sum(yc * yc, hr_ref, he_ref) * inv_n
    yn = yc * lax.rsqrt(var + GN_EPS) * lg_ref[...] + lb_ref[...]
    rv = r_ref[...]
    bonus = _head_sum(rv * k0_ref[...] * rk_ref[...] + rv * k1_ref[...] * rk_ref[...], hr_ref, he_ref)
    out = (yn + bonus * v_ref[...]) * _silu(z_ref[...])
    o_ref[...] = x_ref[...] + mod_ref[2:3, :] * _bdot(out, w_ref[...])


def _lru_scan_kernel(u_ref, up_ref, un_ref, cw_ref, cb_ref, gw_ref, gb_ref, lam_ref, y_ref,
                     ext, a_s, b_s, h_s, *, nt, nx, reverse):
    s = pl.program_id(1)
    tile = _tile_of_step(s, nt, nx, reverse)
    seg_first = jnp.logical_or(tile == 0, tile == nx)
    seg_last = jnp.logical_or(tile == nx - 1, tile == nt - 1)
    halo = SUBLANES

    @pl.when(s == 0)
    def _():
        h_s[...] = jnp.zeros_like(h_s)

    ext[0:halo, :] = jnp.where(seg_first, 0.0, up_ref[...])
    ext[halo:halo + TM, :] = u_ref[...]
    ext[halo + TM:2 * halo + TM, :] = jnp.where(seg_last, 0.0, un_ref[...])
    left = CONV_W // 2
    ev = ext[...]
    nrow = ev.shape[0]
    cv = cb_ref[...]
    for j in range(CONV_W):
        tap = ev if j == left else pltpu.roll(ev, (left - j) % nrow, 0)
        cv = cv + tap[halo:halo + TM, :] * cw_ref[j:j + 1, :]

    nblk = cv.shape[1] // LRU_BS
    neg_c_sp = -LRU_C * _softplus(-lam_ref[...])
    for n in range(nblk):
        sl = slice(n * LRU_BS, (n + 1) * LRU_BS)
        blk = cv[:, sl]
        gates = _bdot(blk, gw_ref[n])
        r = _sigmoid(gates[:, :LRU_BS] + gb_ref[0:1, sl])
        i = _sigmoid(gates[:, LRU_BS:] + gb_ref[1:2, sl])
        log_a = neg_c_sp[:, sl] * r
        a = jnp.exp(log_a)
        a_s[:, sl] = a
        b_s[:, sl] = jnp.sqrt(-jnp.tanh(log_a) * (a * a + 1.0)) * (i * blk)

    def body(j, h):
        t = TM - 1 - j if reverse else j
        h = a_s[pl.ds(t, 1), :] * h + b_s[pl.ds(t, 1), :]
        y_ref[pl.ds(t, 1), :] = h
        return h

    h_s[...] = lax.fori_loop(0, TM, body, h_s[...], unroll=8)


def _lru_scan(uz, conv_w, conv_b, gw, gb, lam, nx, reverse):
    bsz, tt, e2 = uz.shape
    e = e2 // 2
    nt = tt // TM
    r8 = TM // SUBLANES
    nb8 = tt // SUBLANES
    tile = lambda s: _tile_of_step(s, nt, nx, reverse)
    kern = functools.partial(_lru_scan_kernel, nt=nt, nx=nx, reverse=reverse)
    return pl.pallas_call(
        kern,
        grid=(bsz, nt),
        in_specs=[pl.BlockSpec((None, TM, e), lambda b, s: (b, tile(s), 0)),
                  pl.BlockSpec((None, SUBLANES, e), lambda b, s: (b, jnp.maximum(tile(s) * r8 - 1, 0), 0)),
                  pl.BlockSpec((None, SUBLANES, e), lambda b, s: (b, jnp.minimum((tile(s) + 1) * r8, nb8 - 1), 0)),
                  pl.BlockSpec((CONV_W, e), lambda b, s: (0, 0)),
                  pl.BlockSpec((1, e), lambda b, s: (0, 0)),
                  pl.BlockSpec(gw.shape, lambda b, s: (0, 0, 0)),
                  pl.BlockSpec((2, e), lambda b, s: (0, 0)),
                  pl.BlockSpec((1, e), lambda b, s: (0, 0))],
        out_specs=pl.BlockSpec((None, TM, e), lambda b, s: (b, tile(s), 0)),
        out_shape=jax.ShapeDtypeStruct((bsz, tt, e), F32),
        scratch_shapes=[pltpu.VMEM((TM + 2 * SUBLANES, e), F32),
                        pltpu.VMEM((TM, e), F32),
                        pltpu.VMEM((TM, e), F32),
                        pltpu.VMEM((1, e), F32)],
        compiler_params=_cp(("arbitrary", "arbitrary")),
    )(uz, uz, uz, conv_w, conv_b, gw, gb, lam)


def _lru_layer(xs, mods, norm_g, w_in, conv_w, conv_b, gate_w, gate_b, lam, w_out, nx, final_g=None):
    bsz, tt, d = xs.shape
    e = w_out.shape[0]
    nblk = e // LRU_BS
    uz = _nm_matmul(xs, mods, norm_g.reshape(1, d), w_in.astype(BF16),
                    jnp.zeros((1, 2 * e), F32), e, nx)
    ys = []
    for dr in range(2):
        gw = jnp.transpose(gate_w[dr], (1, 2, 0, 3)).reshape(nblk, LRU_BS, 2 * LRU_BS).astype(BF16)
        ys.append(_lru_scan(uz, conv_w, conv_b.reshape(1, e), gw, gate_b[dr], lam[dr].reshape(1, e),
                            nx, reverse=(dr == 1)))
    in_specs = [pl.BlockSpec((None, TB, d), lambda b, i: (b, i, 0)),
                _mod_spec(d, nx, TB),
                pl.BlockSpec((None, TB, e), lambda b, i: (b, i, 0)),
                pl.BlockSpec((None, TB, e), lambda b, i: (b, i, 0)),
                pl.BlockSpec((None, TB, e), lambda b, i: (b, i, 1)),
                pl.BlockSpec((e, d), lambda b, i: (0, 0))]
    args = [xs, mods, ys[0], ys[1], uz, w_out.astype(BF16)]
    if final_g is None:
        kern, rows_out = _lru_out_kernel, tt
    else:
        kern, rows_out = _lru_out_final_kernel, nx * TM
        in_specs.append(pl.BlockSpec((1, d), lambda b, i: (0, 0)))
        args.append(final_g.reshape(1, d))
    return pl.pallas_call(
        kern,
        grid=(bsz, pl.cdiv(rows_out, TB)),
        in_specs=in_specs,
        out_specs=pl.BlockSpec((None, TB, d), lambda b, i: (b, i, 0)),
        out_shape=jax.ShapeDtypeStruct((bsz, rows_out, d), F32),
        compiler_params=_cp(("arbitrary", "arbitrary")),
    )(*args)


def _mlstm_scan_kernel(q_ref, k_ref, v_ref, g_ref, it_ref, ft_ref, o_ref, c_s, n_s, m_s,
                       *, dr, reverse):
    hg = pl.program_id(1)
    s = pl.program_id(2)
    L = MLSTM_CHUNK
    DK, DV = MLSTM_DK, MLSTM_DV
    nchunk = TM // L
    ng = q_ref.shape[1] // DK

    @pl.when(s == 0)
    def _():
        c_s[...] = jnp.zeros_like(c_s)
        n_s[...] = jnp.zeros_like(n_s)
        m_s[...] = jnp.zeros_like(m_s)

    row = lax.broadcasted_iota(jnp.int32, (L, L), 0)
    col = lax.broadcasted_iota(jnp.int32, (L, L), 1)
    causal = (col >= row) if reverse else (col <= row)
    causal_t = (row >= col) if reverse else (row <= col)
    lane = lax.broadcasted_iota(jnp.int32, (TM, LANES), 1)
    last = 0 if reverse else L - 1
    g = g_ref[...]
    g_ls = _log_sigmoid(g)
    f_rows = _log_sigmoid(ft_ref[...])

    inst = []
    q16, bcum, m_in, pv, d1, b_last, m_loc, kv, ks, qf = [], [], [], [], [], [], [], [], [], []
    for gi in range(ng):
        i_lane = dr * 2 * MLSTM_HEADS + hg * ng + gi
        i_col_t = jnp.sum(jnp.where(lane == i_lane, g, 0.0), axis=1, keepdims=True)
        f_col_t = jnp.sum(jnp.where(lane == i_lane + MLSTM_HEADS, g_ls, 0.0), axis=1, keepdims=True)
        for c in range(nchunk):
            rows = slice(c * L, (c + 1) * L)
            i_row = it_ref[gi, c:c + 1, :]
            f_row = f_rows[gi, c:c + 1, :]
            bc = jnp.sum(jnp.where(causal, f_row, 0.0), axis=1, keepdims=True)
            bcum_row = jnp.sum(jnp.where(causal_t, f_col_t[rows], 0.0), axis=0, keepdims=True)
            logw = jnp.where(causal, bc - bcum_row + i_row, -jnp.inf)
            mi = jnp.max(logw, axis=1, keepdims=True)
            q = q_ref[rows, gi * DK:(gi + 1) * DK]
            k = k_ref[rows, gi * DK:(gi + 1) * DK] * (DK ** -0.5)
            v = v_ref[rows, gi * DV:(gi + 1) * DV].astype(BF16)
            sm = _dot_nt(q.astype(BF16), k.astype(BF16)) * jnp.exp(logw - mi)
            bl = bc[last:last + 1, :]
            tot = bl - bc + i_col_t[rows]
            ml = jnp.max(tot, axis=0, keepdims=True)
            kw = k * jnp.exp(tot - ml)
            inst.append((c, gi))
            qf.append(q)
            q16.append(q.astype(BF16))
            bcum.append(bc)
            m_in.append(mi)
            pv.append(_bdot(sm, v))
            d1.append(jnp.sum(sm, axis=1, keepdims=True))
            b_last.append(bl)
            m_loc.append(ml)
            kv.append(_dot_tn(kw.astype(BF16), v))
            ks.append(jnp.sum(kw, axis=0, keepdims=True))

    for j in range(nchunk):
        c = nchunk - 1 - j if reverse else j
        rows = slice(c * L, (c + 1) * L)
        ids = [inst.index((c, gi)) for gi in range(ng)]
        qc = [_bdot(q16[i], c_s[gi]) for gi, i in enumerate(ids)]
        qn = [jnp.sum(qf[i] * n_s[gi], axis=1, keepdims=True) for gi, i in enumerate(ids)]
        for gi, i in enumerate(ids):
            m_prev = m_s[gi]
            gq = bcum[i] + m_prev
            m_t = jnp.maximum(gq, m_in[i])
            sc_in = jnp.exp(m_in[i] - m_t)
            inter = jnp.exp(gq - m_t)
            den = sc_in * d1[i] + inter * qn[gi]
            rden = 1.0 / jnp.maximum(jnp.abs(den), jnp.exp(-m_t))
            o_ref[rows, gi * DV:(gi + 1) * DV] = ((sc_in * rden) * pv[i] + (inter * rden) * qc[gi])
            m_new = jnp.maximum(b_last[i] + m_prev, m_loc[i])
            decay = jnp.exp(b_last[i] + m_prev - m_new)
            sc = jnp.exp(m_loc[i] - m_new)
            c_s[gi] = decay * c_s[gi] + sc * kv[i]
            n_s[gi] = decay * n_s[gi] + sc * ks[i]
            m_s[gi] = m_new


MLSTM_GROUP = 8


def _mlstm_scan(qkvz, g, gt, nx, dr):
    bsz, tt, _ = qkvz.shape
    nt = tt // TM
    reverse = dr == 1
    nh = MLSTM_HEADS
    ng = MLSTM_GROUP
    nchunk = TM // MLSTM_CHUNK
    tile = lambda s: _tile_of_step(s, nt, nx, reverse)
    kern = functools.partial(_mlstm_scan_kernel, dr=dr, reverse=reverse)
    koff = nh // ng
    voff = 2 * nh * MLSTM_DK // (ng * MLSTM_DV)
    ioff = dr * 2 * nh // ng
    foff = (dr * 2 * nh + nh) // ng
    return pl.pallas_call(
        kern,
        grid=(bsz, nh // ng, nt),
        in_specs=[pl.BlockSpec((None, TM, ng * MLSTM_DK), lambda b, h, s: (b, tile(s), h)),
                  pl.BlockSpec((None, TM, ng * MLSTM_DK), lambda b, h, s: (b, tile(s), koff + h)),
                  pl.BlockSpec((None, TM, ng * MLSTM_DV), lambda b, h, s: (b, tile(s), voff + h)),
                  pl.BlockSpec((None, TM, LANES), lambda b, h, s: (b, tile(s), 0)),
                  pl.BlockSpec((None, ng, None, nchunk, MLSTM_CHUNK),
                               lambda b, h, s: (b, ioff + h, tile(s), 0, 0)),
                  pl.BlockSpec((None, ng, None, nchunk, MLSTM_CHUNK),
                               lambda b, h, s: (b, foff + h, tile(s), 0, 0))],
        out_specs=pl.BlockSpec((None, TM, ng * MLSTM_DV), lambda b, h, s: (b, tile(s), h)),
        out_shape=jax.ShapeDtypeStruct((bsz, tt, nh * MLSTM_DV), F32),
        scratch_shapes=[pltpu.VMEM((ng, MLSTM_DK, MLSTM_DV), F32),
                        pltpu.VMEM((ng, 1, MLSTM_DK), F32),
                        pltpu.VMEM((ng, 1, 1), F32)],
        compiler_params=_cp(("arbitrary", "arbitrary", "arbitrary")),
    )(qkvz, qkvz, qkvz, g, gt, gt)


def _mlstm_layer(xs, mods, norm_g, w_in, gate_b, hnorm_g, w_out, nx):
    bsz, tt, d = xs.shape
    nh = MLSTM_HEADS
    qk = nh * MLSTM_DK
    width = nh * MLSTM_DV
    nmain = 2 * qk + 2 * width
    ng = 4 * nh
    g2 = norm_g.reshape(1, d)
    qkvz = _nm_matmul(xs, mods, g2, w_in[:, :nmain].astype(BF16), jnp.zeros((1, nmain), F32),
                      width, nx)
    wg = jnp.pad(w_in[:, nmain:], ((0, 0), (0, LANES - ng))).astype(BF16)
    bg = jnp.pad(gate_b.reshape(1, ng), ((0, 0), (0, LANES - ng)))
    g = _nm_matmul(xs, mods, g2, wg, bg, LANES, nx)
    gt = jnp.transpose(g[:, :, :ng], (0, 2, 1)).reshape(bsz, ng, tt // TM, TM // MLSTM_CHUNK, MLSTM_CHUNK)
    hs = [_mlstm_scan(qkvz, g, gt, nx, dr) for dr in range(2)]
    return pl.pallas_call(
        _mlstm_out_kernel,
        grid=(bsz, pl.cdiv(tt, TB)),
        in_specs=[pl.BlockSpec((None, TB, d), lambda b, i: (b, i, 0)),
                  _mod_spec(d, nx, TB),
                  pl.BlockSpec((None, TB, width), lambda b, i: (b, i, 0)),
                  pl.BlockSpec((None, TB, width), lambda b, i: (b, i, 0)),
                  pl.BlockSpec((None, TB, width), lambda b, i: (b, i, (2 * qk + width) // width)),
                  pl.BlockSpec((1, width), lambda b, i: (0, 0)),
                  pl.BlockSpec((width, d), lambda b, i: (0, 0))],
        out_specs=pl.BlockSpec((None, TB, d), lambda b, i: (b, i, 0)),
        out_shape=jax.ShapeDtypeStruct(xs.shape, F32),
        scratch_shapes=[pltpu.VMEM((TB, width), F32)],
        compiler_params=_cp(("arbitrary", "arbitrary")),
    )(xs, mods, hs[0], hs[1], qkvz, hnorm_g.reshape(1, width), w_out.astype(BF16))


def _rwkv_prep_kernel(x_ref, xp_ref, xn_ref, mod_ref, g_ref, mu_ref, wr_ref, w0_ref, w1_ref, w2_ref,
                      a0_ref, a1_ref, a2_ref, kk_ref, ka_ref, hr_ref, he_ref,
                      r_o, kn_o, v_o, z_o, lw0_o, lw1_o, be0_o, be1_o, k0_o, k1_o,
                      hs, sh, *, nt, nx):
    i = pl.program_id(1)
    seg_first = jnp.logical_or(i == 0, i == nx)
    seg_last = jnp.logical_or(i == nx - 1, i == nt - 1)
    gw = GRID_W
    d = x_ref.shape[1]
    g = g_ref[...]
    hs[0:gw, :] = jnp.where(seg_first, 0.0, _normmod(xp_ref[...], g, mod_ref))
    hs[gw:gw + TM, :] = _normmod(x_ref[...], g, mod_ref)
    hs[gw + TM:2 * gw + TM, :] = jnp.where(seg_last, 0.0, _normmod(xn_ref[...], g, mod_ref))

    @pl.when(i >= nx)
    def _():
        half = d // 2
        sh[:, 0:half] = hs[pl.ds(gw - 1, TM), 0:half]
        sh[:, half:d] = hs[pl.ds(gw + 1, TM), half:d]

    @pl.when(i < nx)
    def _():
        q = d // 4
        wcol = lax.broadcasted_iota(jnp.int32, (TM, q), 0) % gw
        sh[:, 0:q] = jnp.where(wcol == 0, 0.0, hs[pl.ds(gw - 1, TM), 0:q])
        sh[:, q:2 * q] = jnp.where(wcol == gw - 1, 0.0, hs[pl.ds(gw + 1, TM), q:2 * q])
        sh[:, 2 * q:3 * q] = hs[pl.ds(0, TM), 2 * q:3 * q]
        sh[:, 3 * q:d] = hs[pl.ds(2 * gw, TM), 3 * q:d]

    h = hs[pl.ds(gw, TM), :]
    dx = sh[...] - h
    mix = lambda n: h + dx * mu_ref[n:n + 1, :]
    r = _bdot(mix(0), wr_ref[0])
    k = _bdot(mix(1), wr_ref[1])
    r_o[...] = r
    v_o[...] = _bdot(mix(2), wr_ref[2])
    z_o[...] = _bdot(mix(3), wr_ref[3])
    kk = k * kk_ref[...]
    ss = _head_sum(kk * kk, hr_ref, he_ref)
    kn = kk / jnp.maximum(jnp.sqrt(ss), 1e-12)
    kn_o[...] = kn
    xw = mix(4)
    xa = mix(5)
    for dr, (lw_o, be_o, k_o) in enumerate(((lw0_o, be0_o, k0_o), (lw1_o, be1_o, k1_o))):
        w_pre = w0_ref[dr:dr + 1, :] + _bdot(jnp.tanh(_bdot(xw, w1_ref[dr])), w2_ref[dr])
        lw_o[...] = -jnp.exp(-_softplus(-w_pre) - 0.5)
        a = _sigmoid(a0_ref[dr:dr + 1, :] + _bdot(_bdot(xa, a1_ref[dr]), a2_ref[dr]))
        be_o[...] = a * kn
        k_o[...] = k * (1.0 + (a - 1.0) * ka_ref[...])


def _chunk_cumsum(x, chunk, reverse):
    n = x.shape[0]
    pos = lax.broadcasted_iota(jnp.int32, x.shape, 0) % chunk
    sh = 1
    while sh < chunk:
        if reverse:
            x = x + jnp.where(pos < chunk - sh, pltpu.roll(x, n - sh, 0), 0.0)
        else:
            x = x + jnp.where(pos >= sh, pltpu.roll(x, sh, 0), 0.0)
        sh *= 2
    return x


def _bd(p):
    lane = lax.broadcasted_iota(jnp.int32, p.shape, 1)
    zero = jnp.zeros_like(p)
    return jnp.concatenate([jnp.where(lane < RWKV_HEAD, p, zero),
                            jnp.where(lane >= RWKV_HEAD, p, zero)], axis=0)


def _pdot(a, b):
    return jnp.dot(a.astype(BF16), _bd(b.astype(BF16)), preferred_element_type=F32)


def _pdot_tn(a, b):
    full = _dot_tn(a.astype(BF16), b.astype(BF16))
    lane = lax.broadcasted_iota(jnp.int32, (RWKV_HEAD, 2 * RWKV_HEAD), 1)
    return jnp.where(lane < RWKV_HEAD, full[:RWKV_HEAD], full[RWKV_HEAD:])


def _unit_lower_inverses(ns, strict, eye, blk):
    L = ns[0].shape[0]
    row = lax.broadcasted_iota(jnp.int32, ns[0].shape, 0)
    col = lax.broadcasted_iota(jnp.int32, ns[0].shape, 1) % L
    same = (row // blk) == (col // blk)
    in_blk = jnp.logical_and(strict, same)
    off_blk = jnp.logical_and(strict, jnp.logical_not(same))
    pws = [jnp.where(in_blk, n, 0.0) for n in ns]
    tds = [eye - p for p in pws]
    for _ in range(blk.bit_length() - 2):
        pws = [_pdot(p, p) for p in pws]
        tds = [_pdot(t, eye + p) for t, p in zip(tds, pws)]
    pws = [_pdot(t, jnp.where(off_blk, n, 0.0)) for t, n in zip(tds, ns)]
    tis = [eye - p for p in pws]
    for _ in range((L // blk).bit_length() - 2):
        pws = [_pdot(p, p) for p in pws]
        tis = [_pdot(t, eye + p) for t, p in zip(tis, pws)]
    return [_pdot(ti, td) for ti, td in zip(tis, tds)]


def _rwkv_scan_kernel(r_ref, kn_ref, v_ref, lw_ref, be_ref, kd_ref, y_ref,
                      st_s, rp_s, y0_s, m_s, z_s, pl_s, *, reverse):
    s = pl.program_id(2)
    L = RWKV_CHUNK
    N = RWKV_HEAD
    PW = 2 * N
    assert L == N
    nchunk = TM // L
    npair = r_ref.shape[1] // PW

    @pl.when(s == 0)
    def _():
        st_s[...] = jnp.zeros_like(st_s)

    row = lax.broadcasted_iota(jnp.int32, (L, PW), 0)
    col = lax.broadcasted_iota(jnp.int32, (L, PW), 1) % L
    incl = (col >= row) if reverse else (col <= row)
    strict = (col > row) if reverse else (col < row)
    eye = (col == row).astype(F32)
    last = 0 if reverse else L - 1

    lw_all = lw_ref[...]
    lc_all = _chunk_cumsum(lw_all, L, reverse)

    a_p, r_p, b_i, k_i, b_l, k_l, vh, where = [], [], [], [], [], [], [], []
    for c in range(nchunk):
        rows = slice(c * L, (c + 1) * L)
        lw = lw_all[rows]
        lc = lc_all[rows]
        lc_last = lc[last:last + 1, :]
        einv = jnp.exp(-lc)
        to_end = jnp.exp(lc_last - lc)
        be = be_ref[rows, :]
        kd = kd_ref[rows, :]
        full = [(kn_ref[rows, :] * jnp.exp(lc - lw)).astype(BF16),
                (r_ref[rows, :] * jnp.exp(lc)).astype(BF16),
                (be * einv).astype(BF16), (kd * einv).astype(BF16),
                (be * to_end).astype(BF16), (kd * to_end).astype(BF16),
                v_ref[rows, :].astype(BF16)]
        pl_s[c] = jnp.exp(lc_last)
        for pp in range(npair):
            sl = slice(pp * PW, (pp + 1) * PW)
            for dst, src in zip((a_p, r_p, b_i, k_i, b_l, k_l, vh), full):
                dst.append(src[:, sl])
            where.append((c, rows, pp, sl))
    n_inst = len(where)
    ams = [_dot_nt(jnp.concatenate([a_p[i], r_p[i]], axis=0),
                   jnp.concatenate([_bd(b_i[i]), _bd(k_i[i])], axis=0)) for i in range(n_inst)]
    tinvs = _unit_lower_inverses([am[:L, :PW] for am in ams], strict, eye, 16)
    avs = [_pdot(jnp.where(strict, ams[i][:L, PW:], 0.0), vh[i]) for i in range(n_inst)]
    xa = [_pdot(tinvs[i], a_p[i]) for i in range(n_inst)]
    xw = [_pdot(tinvs[i], avs[i]) for i in range(n_inst)]
    rks = [_pdot(jnp.where(incl, ams[i][L:, PW:], 0.0), vh[i]) for i in range(n_inst)]
    rbs = [jnp.where(incl, ams[i][L:, :PW], 0.0) for i in range(n_inst)]
    rpv = [r_p[i].astype(F32) - _pdot(rbs[i], xa[i]) for i in range(n_inst)]
    y0v = [rks[i] - _pdot(rbs[i], xw[i]) for i in range(n_inst)]
    mv = [_pdot_tn(xa[i], b_l[i]) for i in range(n_inst)]
    zv = [_pdot_tn(vh[i], k_l[i]) - _pdot_tn(xw[i], b_l[i]) for i in range(n_inst)]
    for i, (c, rows, pp, sl) in enumerate(where):
        rp_s[rows, sl] = rpv[i]
        y0_s[rows, sl] = y0v[i]
        m_s[c, pp] = -mv[i]
        z_s[c, pp] = zv[i]

    for j in range(nchunk):
        c = nchunk - 1 - j if reverse else j
        rows = slice(c * L, (c + 1) * L)
        p_l = pl_s[c]
        for pp in range(npair):
            sl = slice(pp * PW, (pp + 1) * PW)
            st = st_s[pp]
            y_ref[rows, sl] = _dot_nt(rp_s[rows, sl].astype(BF16), _bd(st.astype(BF16))) + y0_s[rows, sl]
            st_s[pp] = st * p_l[:, sl] + _pdot(st, m_s[c, pp]) + z_s[c, pp]


RWKV_GROUP = 8


def _rwkv_scan(r, kn, v, lw, be, kd, nx, reverse):
    bsz, tt, w = r.shape
    nt = tt // TM
    tile = lambda s: _tile_of_step(s, nt, nx, reverse)
    nh = RWKV_GROUP
    gw = nh * RWKV_HEAD
    spec = pl.BlockSpec((None, TM, gw), lambda b, p, s: (b, tile(s), p))
    kern = functools.partial(_rwkv_scan_kernel, reverse=reverse)
    nchunk = TM // RWKV_CHUNK
    return pl.pallas_call(
        kern,
        grid=(bsz, w // gw, nt),
        in_specs=[spec] * 6,
        out_specs=spec,
        out_shape=jax.ShapeDtypeStruct((bsz, tt, w), F32),
        scratch_shapes=[pltpu.VMEM((nh // 2, RWKV_HEAD, 2 * RWKV_HEAD), F32),
                        pltpu.VMEM((TM, gw), F32),
                        pltpu.VMEM((TM, gw), F32),
                        pltpu.VMEM((nchunk, nh // 2, RWKV_HEAD, 2 * RWKV_HEAD), F32),
                        pltpu.VMEM((nchunk, nh // 2, RWKV_HEAD, 2 * RWKV_HEAD), F32),
                        pltpu.VMEM((nchunk, 1, gw), F32)],
        compiler_params=_cp(("arbitrary", "arbitrary", "arbitrary")),
    )(r, kn, v, lw, be, kd)


def _rwkv_layer(xs, mods, norm_g, mu, w_rkvz, w0, w1, w2, a0, a1, a2, k_k, k_a, r_k, ln_g, ln_b,
                w_out, nx):
    bsz, tt, d = xs.shape
    nt = tt // TM
    r64 = TM // GRID_W
    nb64 = tt // GRID_W
    hred = (jnp.arange(d)[:, None] // RWKV_HEAD == jnp.arange(LANES)[None, :]).astype(BF16)
    hexp = hred.T
    row = lambda a: a.reshape(1, d)
    full = lambda a: pl.BlockSpec(a.shape, lambda b, i: (0,) * a.ndim)
    tile_spec = pl.BlockSpec((None, TM, d), lambda b, i: (b, i, 0))
    big_spec = pl.BlockSpec((None, TB, d), lambda b, i: (b, i, 0))
    consts = [row(norm_g), mu, w_rkvz.astype(BF16), w0, w1.astype(BF16), w2.astype(BF16),
              a0, a1.astype(BF16), a2.astype(BF16), row(k_k), row(k_a), hred, hexp]
    kern = functools.partial(_rwkv_prep_kernel, nt=nt, nx=nx)
    outs = pl.pallas_call(
        kern,
        grid=(bsz, nt),
        in_specs=[tile_spec,
                  pl.BlockSpec((None, GRID_W, d), lambda b, i: (b, jnp.maximum(i * r64 - 1, 0), 0)),
                  pl.BlockSpec((None, GRID_W, d), lambda b, i: (b, jnp.minimum((i + 1) * r64, nb64 - 1), 0)),
                  _mod_spec(d, nx, TM)] + [full(a) for a in consts],
        out_specs=[tile_spec] * 10,
        out_shape=[jax.ShapeDtypeStruct(xs.shape, F32)] * 10,
        scratch_shapes=[pltpu.VMEM((TM + 2 * GRID_W, d), F32), pltpu.VMEM((TM, d), F32)],
        compiler_params=_cp(("arbitrary", "arbitrary")),
    )(xs, xs, xs, mods, *consts)
    r, kn, v, z, lw0, lw1, be0, be1, k0, k1 = outs
    y0 = _rwkv_scan(r, kn, v, lw0, be0, k0, nx, reverse=False)
    y1 = _rwkv_scan(r, kn, v, lw1, be1, k1, nx, reverse=True)
    fin = [row(r_k), row(ln_g), row(ln_b), hred, hexp, w_out.astype(BF16)]
    return pl.pallas_call(
        _rwkv_out_kernel,
        grid=(bsz, pl.cdiv(tt, TB)),
        in_specs=[big_spec, _mod_spec(d, nx, TB)] + [big_spec] * 7 + [full(a) for a in fin],
        out_specs=big_spec,
        out_shape=jax.ShapeDtypeStruct(xs.shape, F32),
        compiler_params=_cp(("arbitrary", "arbitrary")),
    )(xs, mods, y0, y1, r, k0, k1, v, z, *fin)


def _final_norm_kernel(x_ref, g_ref, o_ref):
    xv = x_ref[...]
    o_ref[...] = xv * lax.rsqrt(jnp.mean(xv * xv, axis=-1, keepdims=True) + NORM_EPS) * g_ref[...]


def _final_norm(xs, g, t):
    bsz, _, d = xs.shape
    return pl.pallas_call(
        _final_norm_kernel,
        grid=(bsz, t // TB),
        in_specs=[pl.BlockSpec((None, TB, d), lambda b, i: (b, i, 0)),
                  pl.BlockSpec((1, d), lambda b, i: (0, 0))],
        out_specs=pl.BlockSpec((None, TB, d), lambda b, i: (b, i, 0)),
        out_shape=jax.ShapeDtypeStruct((bsz, t, d), F32),
        compiler_params=_cp(("arbitrary", "arbitrary")),
    )(xs, g.reshape(1, d))


def kernel(x, c, ctx, c_ctx, norm_g, mod_w, mod_b, final_g, lru_w_in, lru_conv_w, lru_conv_b, lru_gate_w, lru_gate_b, lru_lam, lru_w_out, mlstm_w_in, mlstm_gate_b, mlstm_norm_g, mlstm_w_out, r7_mu, r7_w_rkvz, r7_w0, r7_w1, r7_w2, r7_a0, r7_a1, r7_a2, r7_k_k, r7_k_a, r7_r_k, r7_ln_g, r7_ln_b, r7_w_out):
    bsz, t, d = x.shape
    ctx_len = ctx.shape[1]
    depth = norm_g.shape[0]
    assert ctx_len % TM == 0 and t % TB == 0 and TB % TM == 0 and t % GRID_W == 0 and bsz < SUBLANES
    nx = t // TM

    cc = jnp.zeros((SUBLANES, d), F32).at[:bsz].set(c).at[bsz].set(c_ctx)
    mp = _mod_params(cc, mod_w, mod_b).reshape(depth, SUBLANES, 3, d)
    mods_all = jnp.stack([jnp.broadcast_to(mp[:, bsz][:, None], (depth, bsz, 3, d)), mp[:, :bsz]],
                         axis=2).reshape(depth, 2 * bsz, 3, d)

    xs = jnp.concatenate([x, ctx], axis=1)
    for i in range(depth):
        kind, j = i % 3, i // 3
        mods = mods_all[i]
        if kind == 0:
            xs = _lru_layer(xs, mods, norm_g[i], lru_w_in[j], lru_conv_w[j], lru_conv_b[j],
                            lru_gate_w[j], lru_gate_b[j], lru_lam[j], lru_w_out[j], nx,
                            final_g=final_g if i == depth - 1 else None)
        elif kind == 1:
            xs = _mlstm_layer(xs, mods, norm_g[i], mlstm_w_in[j], mlstm_gate_b[j], mlstm_norm_g[j],
                              mlstm_w_out[j], nx)
        else:
            xs = _rwkv_layer(xs, mods, norm_g[i], r7_mu[j], r7_w_rkvz[j], r7_w0[j], r7_w1[j], r7_w2[j],
                             r7_a0[j], r7_a1[j], r7_a2[j], r7_k_k[j], r7_k_a[j], r7_r_k[j],
                             r7_ln_g[j], r7_ln_b[j], r7_w_out[j], nx)
    if depth % 3 == 1:
        return xs
    return _final_norm(xs, final_g, t)
```

```python
import functools

import jax
import jax.numpy as jnp
from jax import lax
from jax.experimental import pallas as pl
from jax.experimental.pallas import tpu as pltpu

F32 = jnp.float32
BF16 = jnp.bfloat16

GRID_W = 64
LRU_C = 8.0
LRU_BS = 128
CONV_W = 4
MLSTM_HEADS = 8
MLSTM_DK = 128
MLSTM_DV = 256
MLSTM_CHUNK = 64
RWKV_HEAD = 64
RWKV_CHUNK = 64
GN_EPS = 64e-5
NORM_EPS = 1e-6

TM = 256
TB = 512
SUBLANES = 8
LANES = 128
VMEM_LIMIT = 56 * 1024 * 1024


def _cp(sem):
    return pltpu.CompilerParams(dimension_semantics=sem, vmem_limit_bytes=VMEM_LIMIT)


def _sigmoid(v):
    return 0.5 * jnp.tanh(0.5 * v) + 0.5


def _silu(v):
    return v * _sigmoid(v)


def _softplus(v):
    return jnp.maximum(v, 0.0) + jnp.log1p(jnp.exp(-jnp.abs(v)))


def _log_sigmoid(v):
    return jnp.minimum(v, 0.0) - jnp.log1p(jnp.exp(-jnp.abs(v)))


def _bdot(a, b):
    return jnp.dot(a.astype(BF16), b.astype(BF16), preferred_element_type=F32)


def _dot_nt(a, b):
    return lax.dot_general(a, b, (((1,), (1,)), ((), ())), preferred_element_type=F32)


def _dot_tn(a, b):
    return lax.dot_general(a, b, (((0,), (0,)), ((), ())), preferred_element_type=F32)


def _normmod(xv, g, mod_ref):
    y = xv * lax.rsqrt(jnp.mean(xv * xv, axis=-1, keepdims=True) + NORM_EPS)
    return (y * g) * (1.0 + mod_ref[1:2, :]) + mod_ref[0:1, :]


def _tile_of_step(s, nt, nx, reverse):
    nct = nt - nx
    if not reverse:
        return jnp.where(s < nct, nx + s, s - nct)
    return jnp.where(s < nct, nt - 1 - s, nx - 1 - (s - nct))


def _mod_row(b, i, nxb):
    return b * 2 + jnp.where(i >= nxb, 0, 1)


def _mod_spec(d, nx, rows):
    nxb = nx * TM // rows
    return pl.BlockSpec((None, 3, d), lambda b, i: (_mod_row(b, i, nxb), 0, 0))


def _split3(v):
    hi = v.astype(BF16)
    rest = v - hi.astype(F32)
    mid = rest.astype(BF16)
    lo = (rest - mid.astype(F32)).astype(BF16)
    return hi, mid, lo


def _mod_kernel(c_ref, w_ref, b_ref, o_ref):
    cs = _split3(_silu(c_ref[...]))
    ws = _split3(w_ref[...])
    acc = b_ref[...]
    for i, j in ((2, 0), (1, 1), (0, 2), (1, 0), (0, 1), (0, 0)):
        acc = acc + jnp.dot(cs[i], ws[j], preferred_element_type=F32)
    o_ref[...] = acc


def _mod_params(cc, mod_w, mod_b):
    depth, d, d3 = mod_w.shape
    rows = cc.shape[0]
    return pl.pallas_call(
        _mod_kernel,
        grid=(depth, d3 // d),
        in_specs=[pl.BlockSpec((rows, d), lambda l, j: (0, 0)),
                  pl.BlockSpec((None, d, d), lambda l, j: (l, 0, j)),
                  pl.BlockSpec((None, 1, d), lambda l, j: (l, 0, j))],
        out_specs=pl.BlockSpec((None, rows, d), lambda l, j: (l, 0, j)),
        out_shape=jax.ShapeDtypeStruct((depth, rows, d3), F32),
        compiler_params=_cp(("arbitrary", "arbitrary")),
    )(cc, mod_w, mod_b.reshape(depth, 1, d3))


def _nm_matmul_kernel(x_ref, mod_ref, g_ref, w_ref, b_ref, o_ref):
    h = _normmod(x_ref[...], g_ref[...], mod_ref)
    o_ref[...] = _bdot(h, w_ref[...]) + b_ref[...]


def _nm_matmul(xs, mods, g, w, bias, tn, nx):
    bsz, tt, d = xs.shape
    n = w.shape[1]
    nxb = nx * TM // TB
    return pl.pallas_call(
        _nm_matmul_kernel,
        grid=(n // tn, bsz, pl.cdiv(tt, TB)),
        in_specs=[pl.BlockSpec((None, TB, d), lambda j, b, i: (b, i, 0)),
                  pl.BlockSpec((None, 3, d), lambda j, b, i: (_mod_row(b, i, nxb), 0, 0)),
                  pl.BlockSpec((1, d), lambda j, b, i: (0, 0)),
                  pl.BlockSpec((d, tn), lambda j, b, i: (0, j)),
                  pl.BlockSpec((1, tn), lambda j, b, i: (0, j))],
        out_specs=pl.BlockSpec((None, TB, tn), lambda j, b, i: (b, i, j)),
        out_shape=jax.ShapeDtypeStruct((bsz, tt, n), F32),
        compiler_params=_cp(("arbitrary", "arbitrary", "arbitrary")),
    )(xs, mods, g, w, bias)


def _lru_out_kernel(x_ref, mod_ref, y0_ref, y1_ref, z_ref, w_ref, o_ref):
    y = (y0_ref[...] + y1_ref[...]) * _silu(z_ref[...])
    o_ref[...] = x_ref[...] + mod_ref[2:3, :] * _bdot(y, w_ref[...])


def _lru_out_final_kernel(x_ref, mod_ref, y0_ref, y1_ref, z_ref, w_ref, fg_ref, o_ref):
    y = (y0_ref[...] + y1_ref[...]) * _silu(z_ref[...])
    xv = x_ref[...] + mod_ref[2:3, :] * _bdot(y, w_ref[...])
    o_ref[...] = xv * lax.rsqrt(jnp.mean(xv * xv, axis=-1, keepdims=True) + NORM_EPS) * fg_ref[...]


def _mlstm_out_kernel(x_ref, mod_ref, h0_ref, h1_ref, z_ref, ng_ref, w_ref, o_ref, hn_s):
    hs = h0_ref[...] + h1_ref[...]
    for hd in range(MLSTM_HEADS):
        sl = slice(hd * MLSTM_DV, (hd + 1) * MLSTM_DV)
        hh = hs[:, sl]
        hn_s[:, sl] = hh * lax.rsqrt(jnp.mean(hh * hh, axis=-1, keepdims=True) + NORM_EPS)
    y = (hn_s[...] * ng_ref[...]) * _silu(z_ref[...])
    o_ref[...] = x_ref[...] + mod_ref[2:3, :] * _bdot(y, w_ref[...])


def _head_sum(v, hr_ref, he_ref):
    hr = hr_ref[...]
    he = he_ref[...]
    red = sum(jnp.dot(p, hr, preferred_element_type=F32) for p in _split3(v))
    return sum(jnp.dot(p, he, preferred_element_type=F32) for p in _split3(red))


def _rwkv_out_kernel(x_ref, mod_ref, y0_ref, y1_ref, r_ref, k0_ref, k1_ref, v_ref, z_ref,
                     rk_ref, lg_ref, lb_ref, hr_ref, he_ref, w_ref, o_ref):
    inv_n = 1.0 / RWKV_HEAD
    y = y0_ref[...] + y1_ref[...]
    mean = _head_sum(y, hr_ref, he_ref) * inv_n
    yc = y - mean
    var = _head_sum(yc * yc, hr_ref, he_ref) * inv_n
    yn = yc * lax.rsqrt(var + GN_EPS) * lg_ref[...] + lb_ref[...]
    rv = r_ref[...]
    bonus = _head_sum(rv * k0_ref[...] * rk_ref[...] + rv * k1_ref[...] * rk_ref[...], hr_ref, he_ref)
    out = (yn + bonus * v_ref[...]) * _silu(z_ref[...])
    o_ref[...] = x_ref[...] + mod_ref[2:3, :] * _bdot(out, w_ref[...])


def _lru_scan_kernel(u_ref, up_ref, un_ref, cw_ref, cb_ref, gw_ref, gb_ref, lam_ref, y_ref,
                     ext, a_s, b_s, h_s, *, nt, nx, reverse):
    s = pl.program_id(1)
    tile = _tile_of_step(s, nt, nx, reverse)
    seg_first = jnp.logical_or(tile == 0, tile == nx)
    seg_last = jnp.logical_or(tile == nx - 1, tile == nt - 1)
    halo = SUBLANES

    @pl.when(s == 0)
    def _():
        h_s[...] = jnp.zeros_like(h_s)

    ext[0:halo, :] = jnp.where(seg_first, 0.0, up_ref[...])
    ext[halo:halo + TM, :] = u_ref[...]
    ext[halo + TM:2 * halo + TM, :] = jnp.where(seg_last, 0.0, un_ref[...])
    left = CONV_W // 2
    ev = ext[...]
    nrow = ev.shape[0]
    cv = cb_ref[...]
    for j in range(CONV_W):
        tap = ev if j == left else pltpu.roll(ev, (left - j) % nrow, 0)
        cv = cv + tap[halo:halo + TM, :] * cw_ref[j:j + 1, :]

    nblk = cv.shape[1] // LRU_BS
    neg_c_sp = -LRU_C * _softplus(-lam_ref[...])
    for n in range(nblk):
        sl = slice(n * LRU_BS, (n + 1) * LRU_BS)
        blk = cv[:, sl]
        gates = _bdot(blk, gw_ref[n])
        r = _sigmoid(gates[:, :LRU_BS] + gb_ref[0:1, sl])
        i = _sigmoid(gates[:, LRU_BS:] + gb_ref[1:2, sl])
        log_a = neg_c_sp[:, sl] * r
        a = jnp.exp(log_a)
        a_s[:, sl] = a
        b_s[:, sl] = jnp.sqrt(-jnp.tanh(log_a) * (a * a + 1.0)) * (i * blk)

    def body(j, h):
        t = TM - 1 - j if reverse else j
        h = a_s[pl.ds(t, 1), :] * h + b_s[pl.ds(t, 1), :]
        y_ref[pl.ds(t, 1), :] = h
        return h

    h_s[...] = lax.fori_loop(0, TM, body, h_s[...], unroll=8)


def _lru_scan(uz, conv_w, conv_b, gw, gb, lam, nx, reverse):
    bsz, tt, e2 = uz.shape
    e = e2 // 2
    nt = tt // TM
    r8 = TM // SUBLANES
    nb8 = tt // SUBLANES
    tile = lambda s: _tile_of_step(s, nt, nx, reverse)
    kern = functools.partial(_lru_scan_kernel, nt=nt, nx=nx, reverse=reverse)
    return pl.pallas_call(
        kern,
        grid=(bsz, nt),
        in_specs=[pl.BlockSpec((None, TM, e), lambda b, s: (b, tile(s), 0)),
                  pl.BlockSpec((None, SUBLANES, e), lambda b, s: (b, jnp.maximum(tile(s) * r8 - 1, 0), 0)),
                  pl.BlockSpec((None, SUBLANES, e), lambda b, s: (b, jnp.minimum((tile(s) + 1) * r8, nb8 - 1), 0)),
                  pl.BlockSpec((CONV_W, e), lambda b, s: (0, 0)),
                  pl.BlockSpec((1, e), lambda b, s: (0, 0)),
                  pl.BlockSpec(gw.shape, lambda b, s: (0, 0, 0)),
                  pl.BlockSpec((2, e), lambda b, s: (0, 0)),
                  pl.BlockSpec((1, e), lambda b, s: (0, 0))],
        out_specs=pl.BlockSpec((None, TM, e), lambda b, s: (b, tile(s), 0)),
        out_shape=jax.ShapeDtypeStruct((bsz, tt, e), F32),
        scratch_shapes=[pltpu.VMEM((TM + 2 * SUBLANES, e), F32),
                        pltpu.VMEM((TM, e), F32),
                        pltpu.VMEM((TM, e), F32),
                        pltpu.VMEM((1, e), F32)],
        compiler_params=_cp(("arbitrary", "arbitrary")),
    )(uz, uz, uz, conv_w, conv_b, gw, gb, lam)


def _lru_layer(xs, mods, norm_g, w_in, conv_w, conv_b, gate_w, gate_b, lam, w_out, nx, final_g=None):
    bsz, tt, d = xs.shape
    e = w_out.shape[0]
    nblk = e // LRU_BS
    uz = _nm_matmul(xs, mods, norm_g.reshape(1, d), w_in.astype(BF16),
                    jnp.zeros((1, 2 * e), F32), e, nx)
    ys = []
    for dr in range(2):
        gw = jnp.transpose(gate_w[dr], (1, 2, 0, 3)).reshape(nblk, LRU_BS, 2 * LRU_BS).astype(BF16)
        ys.append(_lru_scan(uz, conv_w, conv_b.reshape(1, e), gw, gate_b[dr], lam[dr].reshape(1, e),
                            nx, reverse=(dr == 1)))
    in_specs = [pl.BlockSpec((None, TB, d), lambda b, i: (b, i, 0)),
                _mod_spec(d, nx, TB),
                pl.BlockSpec((None, TB, e), lambda b, i: (b, i, 0)),
                pl.BlockSpec((None, TB, e), lambda b, i: (b, i, 0)),
                pl.BlockSpec((None, TB, e), lambda b, i: (b, i, 1)),
                pl.BlockSpec((e, d), lambda b, i: (0, 0))]
    args = [xs, mods, ys[0], ys[1], uz, w_out.astype(BF16)]
    if final_g is None:
        kern, rows_out = _lru_out_kernel, tt
    else:
        kern, rows_out = _lru_out_final_kernel, nx * TM
        in_specs.append(pl.BlockSpec((1, d), lambda b, i: (0, 0)))
        args.append(final_g.reshape(1, d))
    return pl.pallas_call(
        kern,
        grid=(bsz, pl.cdiv(rows_out, TB)),
        in_specs=in_specs,
        out_specs=pl.BlockSpec((None, TB, d), lambda b, i: (b, i, 0)),
        out_shape=jax.ShapeDtypeStruct((bsz, rows_out, d), F32),
        compiler_params=_cp(("arbitrary", "arbitrary")),
    )(*args)


def _mlstm_scan_kernel(q_ref, k_ref, v_ref, g_ref, it_ref, ft_ref, o_ref, c_s, n_s, m_s,
                       *, dr, reverse):
    hg = pl.program_id(1)
    s = pl.program_id(2)
    L = MLSTM_CHUNK
    DK, DV = MLSTM_DK, MLSTM_DV
    nchunk = TM // L
    ng = q_ref.shape[1] // DK

    @pl.when(s == 0)
    def _():
        c_s[...] = jnp.zeros_like(c_s)
        n_s[...] = jnp.zeros_like(n_s)
        m_s[...] = jnp.zeros_like(m_s)

    row = lax.broadcasted_iota(jnp.int32, (L, L), 0)
    col = lax.broadcasted_iota(jnp.int32, (L, L), 1)
    causal = (col >= row) if reverse else (col <= row)
    causal_t = (row >= col) if reverse else (row <= col)
    lane = lax.broadcasted_iota(jnp.int32, (TM, LANES), 1)
    last = 0 if reverse else L - 1
    g = g_ref[...]
    g_ls = _log_sigmoid(g)
    f_rows = _log_sigmoid(ft_ref[...])

    inst = []
    q16, bcum, m_in, pv, d1, b_last, m_loc, kv, ks, qf = [], [], [], [], [], [], [], [], [], []
    for gi in range(ng):
        i_lane = dr * 2 * MLSTM_HEADS + hg * ng + gi
        i_col_t = jnp.sum(jnp.where(lane == i_lane, g, 0.0), axis=1, keepdims=True)
        f_col_t = jnp.sum(jnp.where(lane == i_lane + MLSTM_HEADS, g_ls, 0.0), axis=1, keepdims=True)
        for c in range(nchunk):
            rows = slice(c * L, (c + 1) * L)
            i_row = it_ref[gi, c:c + 1, :]
            f_row = f_rows[gi, c:c + 1, :]
            bc = jnp.sum(jnp.where(causal, f_row, 0.0), axis=1, keepdims=True)
            bcum_row = jnp.sum(jnp.where(causal_t, f_col_t[rows], 0.0), axis=0, keepdims=True)
            logw = jnp.where(causal, bc - bcum_row + i_row, -jnp.inf)
            mi = jnp.max(logw, axis=1, keepdims=True)
            q = q_ref[rows, gi * DK:(gi + 1) * DK]
            k = k_ref[rows, gi * DK:(gi + 1) * DK] * (DK ** -0.5)
            v = v_ref[rows, gi * DV:(gi + 1) * DV].astype(BF16)
            sm = _dot_nt(q.astype(BF16), k.astype(BF16)) * jnp.exp(logw - mi)
            bl = bc[last:last + 1, :]
            tot = bl - bc + i_col_t[rows]
            ml = jnp.max(tot, axis=0, keepdims=True)
            kw = k * jnp.exp(tot - ml)
            inst.append((c, gi))
            qf.append(q)
            q16.append(q.astype(BF16))
            bcum.append(bc)
            m_in.append(mi)
            pv.append(_bdot(sm, v))
            d1.append(jnp.sum(sm, axis=1, keepdims=True))
            b_last.append(bl)
            m_loc.append(ml)
            kv.append(_dot_tn(kw.astype(BF16), v))
            ks.append(jnp.sum(kw, axis=0, keepdims=True))

    for j in range(nchunk):
        c = nchunk - 1 - j if reverse else j
        rows = slice(c * L, (c + 1) * L)
        ids = [inst.index((c, gi)) for gi in range(ng)]
        qc = [_bdot(q16[i], c_s[gi]) for gi, i in enumerate(ids)]
        qn = [jnp.sum(qf[i] * n_s[gi], axis=1, keepdims=True) for gi, i in enumerate(ids)]
        for gi, i in enumerate(ids):
            m_prev = m_s[gi]
            gq = bcum[i] + m_prev
            m_t = jnp.maximum(gq, m_in[i])
            sc_in = jnp.exp(m_in[i] - m_t)
            inter = jnp.exp(gq - m_t)
            den = sc_in * d1[i] + inter * qn[gi]
            rden = 1.0 / jnp.maximum(jnp.abs(den), jnp.exp(-m_t))
            o_ref[rows, gi * DV:(gi + 1) * DV] = ((sc_in * rden) * pv[i] + (inter * rden) * qc[gi])
            m_new = jnp.maximum(b_last[i] + m_prev, m_loc[i])
            decay = jnp.exp(b_last[i] + m_prev - m_new)
            sc = jnp.exp(m_loc[i] - m_new)
            c_s[gi] = decay * c_s[gi] + sc * kv[i]
            n_s[gi] = decay * n_s[gi] + sc * ks[i]
            m_s[gi] = m_new


MLSTM_GROUP = 8


def _mlstm_scan(qkvz, g, gt, nx, dr):
    bsz, tt, _ = qkvz.shape
    nt = tt // TM
    reverse = dr == 1
    nh = MLSTM_HEADS
    ng = MLSTM_GROUP
    nchunk = TM // MLSTM_CHUNK
    tile = lambda s: _tile_of_step(s, nt, nx, reverse)
    kern = functools.partial(_mlstm_scan_kernel, dr=dr, reverse=reverse)
    koff = nh // ng
    voff = 2 * nh * MLSTM_DK // (ng * MLSTM_DV)
    ioff = dr * 2 * nh // ng
    foff = (dr * 2 * nh + nh) // ng
    return pl.pallas_call(
        kern,
        grid=(bsz, nh // ng, nt),
        in_specs=[pl.BlockSpec((None, TM, ng * MLSTM_DK), lambda b, h, s: (b, tile(s), h)),
                  pl.BlockSpec((None, TM, ng * MLSTM_DK), lambda b, h, s: (b, tile(s), koff + h)),
                  pl.BlockSpec((None, TM, ng * MLSTM_DV), lambda b, h, s: (b, tile(s), voff + h)),
                  pl.BlockSpec((None, TM, LANES), lambda b, h, s: (b, tile(s), 0)),
                  pl.BlockSpec((None, ng, None, nchunk, MLSTM_CHUNK),
                               lambda b, h, s: (b, ioff + h, tile(s), 0, 0)),
                  pl.BlockSpec((None, ng, None, nchunk, MLSTM_CHUNK),
                               lambda b, h, s: (b, foff + h, tile(s), 0, 0))],
        out_specs=pl.BlockSpec((None, TM, ng * MLSTM_DV), lambda b, h, s: (b, tile(s), h)),
        out_shape=jax.ShapeDtypeStruct((bsz, tt, nh * MLSTM_DV), F32),
        scratch_shapes=[pltpu.VMEM((ng, MLSTM_DK, MLSTM_DV), F32),
                        pltpu.VMEM((ng, 1, MLSTM_DK), F32),
                        pltpu.VMEM((ng, 1, 1), F32)],
        compiler_params=_cp(("arbitrary", "arbitrary", "arbitrary")),
    )(qkvz, qkvz, qkvz, g, gt, gt)


def _mlstm_layer(xs, mods, norm_g, w_in, gate_b, hnorm_g, w_out, nx):
    bsz, tt, d = xs.shape
    nh = MLSTM_HEADS
    qk = nh * MLSTM_DK
    width = nh * MLSTM_DV
    nmain = 2 * qk + 2 * width
    ng = 4 * nh
    g2 = norm_g.reshape(1, d)
    qkvz = _nm_matmul(xs, mods, g2, w_in[:, :nmain].astype(BF16), jnp.zeros((1, nmain), F32),
                      width, nx)
    wg = jnp.pad(w_in[:, nmain:], ((0, 0), (0, LANES - ng))).astype(BF16)
    bg = jnp.pad(gate_b.reshape(1, ng), ((0, 0), (0, LANES - ng)))
    g = _nm_matmul(xs, mods, g2, wg, bg, LANES, nx)
    gt = jnp.transpose(g[:, :, :ng], (0, 2, 1)).reshape(bsz, ng, tt // TM, TM // MLSTM_CHUNK, MLSTM_CHUNK)
    hs = [_mlstm_scan(qkvz, g, gt, nx, dr) for dr in range(2)]
    return pl.pallas_call(
        _mlstm_out_kernel,
        grid=(bsz, pl.cdiv(tt, TB)),
        in_specs=[pl.BlockSpec((None, TB, d), lambda b, i: (b, i, 0)),
                  _mod_spec(d, nx, TB),
                  pl.BlockSpec((None, TB, width), lambda b, i: (b, i, 0)),
                  pl.BlockSpec((None, TB, width), lambda b, i: (b, i, 0)),
                  pl.BlockSpec((None, TB, width), lambda b, i: (b, i, (2 * qk + width) // width)),
                  pl.BlockSpec((1, width), lambda b, i: (0, 0)),
                  pl.BlockSpec((width, d), lambda b, i: (0, 0))],
        out_specs=pl.BlockSpec((None, TB, d), lambda b, i: (b, i, 0)),
        out_shape=jax.ShapeDtypeStruct(xs.shape, F32),
        scratch_shapes=[pltpu.VMEM((TB, width), F32)],
        compiler_params=_cp(("arbitrary", "arbitrary")),
    )(xs, mods, hs[0], hs[1], qkvz, hnorm_g.reshape(1, width), w_out.astype(BF16))


def _rwkv_prep_kernel(x_ref, xp_ref, xn_ref, mod_ref, g_ref, mu_ref, wr_ref, w0_ref, w1_ref, w2_ref,
                      a0_ref, a1_ref, a2_ref, kk_ref, ka_ref, hr_ref, he_ref,
                      r_o, kn_o, v_o, z_o, lw0_o, lw1_o, be0_o, be1_o, k0_o, k1_o,
                      hs, sh, *, nt, nx):
    i = pl.program_id(1)
    seg_first = jnp.logical_or(i == 0, i == nx)
    seg_last = jnp.logical_or(i == nx - 1, i == nt - 1)
    gw = GRID_W
    d = x_ref.shape[1]
    g = g_ref[...]
    hs[0:gw, :] = jnp.where(seg_first, 0.0, _normmod(xp_ref[...], g, mod_ref))
    hs[gw:gw + TM, :] = _normmod(x_ref[...], g, mod_ref)
    hs[gw + TM:2 * gw + TM, :] = jnp.where(seg_last, 0.0, _normmod(xn_ref[...], g, mod_ref))

    @pl.when(i >= nx)
    def _():
        half = d // 2
        sh[:, 0:half] = hs[pl.ds(gw - 1, TM), 0:half]
        sh[:, half:d] = hs[pl.ds(gw + 1, TM), half:d]

    @pl.when(i < nx)
    def _():
        q = d // 4
        wcol = lax.broadcasted_iota(jnp.int32, (TM, q), 0) % gw
        sh[:, 0:q] = jnp.where(wcol == 0, 0.0, hs[pl.ds(gw - 1, TM), 0:q])
        sh[:, q:2 * q] = jnp.where(wcol == gw - 1, 0.0, hs[pl.ds(gw + 1, TM), q:2 * q])
        sh[:, 2 * q:3 * q] = hs[pl.ds(0, TM), 2 * q:3 * q]
        sh[:, 3 * q:d] = hs[pl.ds(2 * gw, TM), 3 * q:d]

    h = hs[pl.ds(gw, TM), :]
    dx = sh[...] - h
    mix = lambda n: h + dx * mu_ref[n:n + 1, :]
    r = _bdot(mix(0), wr_ref[0])
    k = _bdot(mix(1), wr_ref[1])
    r_o[...] = r
    v_o[...] = _bdot(mix(2), wr_ref[2])
    z_o[...] = _bdot(mix(3), wr_ref[3])
    kk = k * kk_ref[...]
    ss = _head_sum(kk * kk, hr_ref, he_ref)
    kn = kk / jnp.maximum(jnp.sqrt(ss), 1e-12)
    kn_o[...] = kn
    xw = mix(4)
    xa = mix(5)
    for dr, (lw_o, be_o, k_o) in enumerate(((lw0_o, be0_o, k0_o), (lw1_o, be1_o, k1_o))):
        w_pre = w0_ref[dr:dr + 1, :] + _bdot(jnp.tanh(_bdot(xw, w1_ref[dr])), w2_ref[dr])
        lw_o[...] = -jnp.exp(-_softplus(-w_pre) - 0.5)
        a = _sigmoid(a0_ref[dr:dr + 1, :] + _bdot(_bdot(xa, a1_ref[dr]), a2_ref[dr]))
        be_o[...] = a * kn
        k_o[...] = k * (1.0 + (a - 1.0) * ka_ref[...])


def _chunk_cumsum(x, chunk, reverse):
    n = x.shape[0]
    pos = lax.broadcasted_iota(jnp.int32, x.shape, 0) % chunk
    sh = 1
    while sh < chunk:
        if reverse:
            x = x + jnp.where(pos < chunk - sh, pltpu.roll(x, n - sh, 0), 0.0)
        else:
            x = x + jnp.where(pos >= sh, pltpu.roll(x, sh, 0), 0.0)
        sh *= 2
    return x


def _bd(p):
    lane = lax.broadcasted_iota(jnp.int32, p.shape, 1)
    zero = jnp.zeros_like(p)
    return jnp.concatenate([jnp.where(lane < RWKV_HEAD, p, zero),
                            jnp.where(lane >= RWKV_HEAD, p, zero)], axis=0)


def _pdot(a, b):
    return jnp.dot(a.astype(BF16), _bd(b.astype(BF16)), preferred_element_type=F32)


def _pdot_tn(a, b):
    full = _dot_tn(a.astype(BF16), b.astype(BF16))
    lane = lax.broadcasted_iota(jnp.int32, (RWKV_HEAD, 2 * RWKV_HEAD), 1)
    return jnp.where(lane < RWKV_HEAD, full[:RWKV_HEAD], full[RWKV_HEAD:])


def _unit_lower_inverses(ns, strict, eye, blk):
    L = ns[0].shape[0]
    row = lax.broadcasted_iota(jnp.int32, ns[0].shape, 0)
    col = lax.broadcasted_iota(jnp.int32, ns[0].shape, 1) % L
    same = (row // blk) == (col // blk)
    in_blk = jnp.logical_and(strict, same)
    off_blk = jnp.logical_and(strict, jnp.logical_not(same))
    pws = [jnp.where(in_blk, n, 0.0) for n in ns]
    tds = [eye - p for p in pws]
    for _ in range(blk.bit_length() - 2):
        pws = [_pdot(p, p) for p in pws]
        tds = [_pdot(t, eye + p) for t, p in zip(tds, pws)]
    pws = [_pdot(t, jnp.where(off_blk, n, 0.0)) for t, n in zip(tds, ns)]
    tis = [eye - p for p in pws]
    for _ in range((L // blk).bit_length() - 2):
        pws = [_pdot(p, p) for p in pws]
        tis = [_pdot(t, eye + p) for t, p in zip(tis, pws)]
    return [_pdot(ti, td) for ti, td in zip(tis, tds)]


def _rwkv_scan_kernel(r_ref, kn_ref, v_ref, lw_ref, be_ref, kd_ref, y_ref,
                      st_s, rp_s, y0_s, m_s, z_s, pl_s, *, reverse):
    s = pl.program_id(2)
    L = RWKV_CHUNK
    N = RWKV_HEAD
    PW = 2 * N
    assert L == N
    nchunk = TM // L
    npair = r_ref.shape[1] // PW

    @pl.when(s == 0)
    def _():
        st_s[...] = jnp.zeros_like(st_s)

    row = lax.broadcasted_iota(jnp.int32, (L, PW), 0)
    col = lax.broadcasted_iota(jnp.int32, (L, PW), 1) % L
    incl = (col >= row) if reverse else (col <= row)
    strict = (col > row) if reverse else (col < row)
    eye = (col == row).astype(F32)
    last = 0 if reverse else L - 1

    lw_all = lw_ref[...]
    lc_all = _chunk_cumsum(lw_all, L, reverse)

    a_p, r_p, b_i, k_i, b_l, k_l, vh, where = [], [], [], [], [], [], [], []
    for c in range(nchunk):
        rows = slice(c * L, (c + 1) * L)
        lw = lw_all[rows]
        lc = lc_all[rows]
        lc_last = lc[last:last + 1, :]
        einv = jnp.exp(-lc)
        to_end = jnp.exp(lc_last - lc)
        be = be_ref[rows, :]
        kd = kd_ref[rows, :]
        full = [(kn_ref[rows, :] * jnp.exp(lc - lw)).astype(BF16),
                (r_ref[rows, :] * jnp.exp(lc)).astype(BF16),
                (be * einv).astype(BF16), (kd * einv).astype(BF16),
                (be * to_end).astype(BF16), (kd * to_end).astype(BF16),
                v_ref[rows, :].astype(BF16)]
        pl_s[c] = jnp.exp(lc_last)
        for pp in range(npair):
            sl = slice(pp * PW, (pp + 1) * PW)
            for dst, src in zip((a_p, r_p, b_i, k_i, b_l, k_l, vh), full):
                dst.append(src[:, sl])
            where.append((c, rows, pp, sl))
    n_inst = len(where)
    ams = [_dot_nt(jnp.concatenate([a_p[i], r_p[i]], axis=0),
                   jnp.concatenate([_bd(b_i[i]), _bd(k_i[i])], axis=0)) for i in range(n_inst)]
    tinvs = _unit_lower_inverses([am[:L, :PW] for am in ams], strict, eye, 16)
    avs = [_pdot(jnp.where(strict, ams[i][:L, PW:], 0.0), vh[i]) for i in range(n_inst)]
    xx = [jnp.dot(tinvs[i].astype(BF16),
                  jnp.concatenate([_bd(a_p[i]), _bd(avs[i].astype(BF16))], axis=1),
                  preferred_element_type=F32) for i in range(n_inst)]
    xa = [x[:, :PW] for x in xx]
    xw = [x[:, PW:] for x in xx]
    rks = [_pdot(jnp.where(incl, ams[i][L:, PW:], 0.0), vh[i]) for i in range(n_inst)]
    rbs = [jnp.where(incl, ams[i][L:, :PW], 0.0) for i in range(n_inst)]
    rr = [jnp.dot(rbs[i].astype(BF16),
                  jnp.concatenate([_bd(xa[i].astype(BF16)), _bd(xw[i].astype(BF16))], axis=1),
                  preferred_element_type=F32) for i in range(n_inst)]
    rpv = [r_p[i].astype(F32) - rr[i][:, :PW] for i in range(n_inst)]
    y0v = [rks[i] - rr[i][:, PW:] for i in range(n_inst)]
    mv = [_pdot_tn(xa[i], b_l[i]) for i in range(n_inst)]
    zv = [_pdot_tn(vh[i], k_l[i]) - _pdot_tn(xw[i], b_l[i]) for i in range(n_inst)]
    for i, (c, rows, pp, sl) in enumerate(where):
        rp_s[rows, sl] = rpv[i]
        y0_s[rows, sl] = y0v[i]
        m_s[c, pp] = -mv[i]
        z_s[c, pp] = zv[i]

    for j in range(nchunk):
        c = nchunk - 1 - j if reverse else j
        rows = slice(c * L, (c + 1) * L)
        p_l = pl_s[c]
        for pp in range(npair):
            sl = slice(pp * PW, (pp + 1) * PW)
            st = st_s[pp]
            y_ref[rows, sl] = _dot_nt(rp_s[rows, sl].astype(BF16), _bd(st.astype(BF16))) + y0_s[rows, sl]
            st_s[pp] = st * p_l[:, sl] + _pdot(st, m_s[c, pp]) + z_s[c, pp]


RWKV_GROUP = 8


def _rwkv_scan(r, kn, v, lw, be, kd, nx, reverse):
    bsz, tt, w = r.shape
    nt = tt // TM
    tile = lambda s: _tile_of_step(s, nt, nx, reverse)
    nh = RWKV_GROUP
    gw = nh * RWKV_HEAD
    spec = pl.BlockSpec((None, TM, gw), lambda b, p, s: (b, tile(s), p))
    kern = functools.partial(_rwkv_scan_kernel, reverse=reverse)
    nchunk = TM // RWKV_CHUNK
    return pl.pallas_call(
        kern,
        grid=(bsz, w // gw, nt),
        in_specs=[spec] * 6,
        out_specs=spec,
        out_shape=jax.ShapeDtypeStruct((bsz, tt, w), F32),
        scratch_shapes=[pltpu.VMEM((nh // 2, RWKV_HEAD, 2 * RWKV_HEAD), F32),
                        pltpu.VMEM((TM, gw), F32),
                        pltpu.VMEM((TM, gw), F32),
                        pltpu.VMEM((nchunk, nh // 2, RWKV_HEAD, 2 * RWKV_HEAD), F32),
                        pltpu.VMEM((nchunk, nh // 2, RWKV_HEAD, 2 * RWKV_HEAD), F32),
                        pltpu.VMEM((nchunk, 1, gw), F32)],
        compiler_params=_cp(("arbitrary", "arbitrary", "arbitrary")),
    )(r, kn, v, lw, be, kd)


def _rwkv_layer(xs, mods, norm_g, mu, w_rkvz, w0, w1, w2, a0, a1, a2, k_k, k_a, r_k, ln_g, ln_b,
                w_out, nx):
    bsz, tt, d = xs.shape
    nt = tt // TM
    r64 = TM // GRID_W
    nb64 = tt // GRID_W
    hred = (jnp.arange(d)[:, None] // RWKV_HEAD == jnp.arange(LANES)[None, :]).astype(BF16)
    hexp = hred.T
    row = lambda a: a.reshape(1, d)
    full = lambda a: pl.BlockSpec(a.shape, lambda b, i: (0,) * a.ndim)
    tile_spec = pl.BlockSpec((None, TM, d), lambda b, i: (b, i, 0))
    big_spec = pl.BlockSpec((None, TB, d), lambda b, i: (b, i, 0))
    consts = [row(norm_g), mu, w_rkvz.astype(BF16), w0, w1.astype(BF16), w2.astype(BF16),
              a0, a1.astype(BF16), a2.astype(BF16), row(k_k), row(k_a), hred, hexp]
    kern = functools.partial(_rwkv_prep_kernel, nt=nt, nx=nx)
    outs = pl.pallas_call(
        kern,
        grid=(bsz, nt),
        in_specs=[tile_spec,
                  pl.BlockSpec((None, GRID_W, d), lambda b, i: (b, jnp.maximum(i * r64 - 1, 0), 0)),
                  pl.BlockSpec((None, GRID_W, d), lambda b, i: (b, jnp.minimum((i + 1) * r64, nb64 - 1), 0)),
                  _mod_spec(d, nx, TM)] + [full(a) for a in consts],
        out_specs=[tile_spec] * 10,
        out_shape=[jax.ShapeDtypeStruct(xs.shape, F32)] * 10,
        scratch_shapes=[pltpu.VMEM((TM + 2 * GRID_W, d), F32), pltpu.VMEM((TM, d), F32)],
        compiler_params=_cp(("arbitrary", "arbitrary")),
    )(xs, xs, xs, mods, *consts)
    r, kn, v, z, lw0, lw1, be0, be1, k0, k1 = outs
    y0 = _rwkv_scan(r, kn, v, lw0, be0, k0, nx, reverse=False)
    y1 = _rwkv_scan(r, kn, v, lw1, be1, k1, nx, reverse=True)
    fin = [row(r_k), row(ln_g), row(ln_b), hred, hexp, w_out.astype(BF16)]
    return pl.pallas_call(
        _rwkv_out_kernel,
        grid=(bsz, pl.cdiv(tt, TB)),
        in_specs=[big_spec, _mod_spec(d, nx, TB)] + [big_spec] * 7 + [full(a) for a in fin],
        out_specs=big_spec,
        out_shape=jax.ShapeDtypeStruct(xs.shape, F32),
        compiler_params=_cp(("arbitrary", "arbitrary")),
    )(xs, mods, y0, y1, r, k0, k1, v, z, *fin)


def _final_norm_kernel(x_ref, g_ref, o_ref):
    xv = x_ref[...]
    o_ref[...] = xv * lax.rsqrt(jnp.mean(xv * xv, axis=-1, keepdims=True) + NORM_EPS) * g_ref[...]


def _final_norm(xs, g, t):
    bsz, _, d = xs.shape
    return pl.pallas_call(
        _final_norm_kernel,
        grid=(bsz, t // TB),
        in_specs=[pl.BlockSpec((None, TB, d), lambda b, i: (b, i, 0)),
                  pl.BlockSpec((1, d), lambda b, i: (0, 0))],
        out_specs=pl.BlockSpec((None, TB, d), lambda b, i: (b, i, 0)),
        out_shape=jax.ShapeDtypeStruct((bsz, t, d), F32),
        compiler_params=_cp(("arbitrary", "arbitrary")),
    )(xs, g.reshape(1, d))


def kernel(x, c, ctx, c_ctx, norm_g, mod_w, mod_b, final_g, lru_w_in, lru_conv_w, lru_conv_b, lru_gate_w, lru_gate_b, lru_lam, lru_w_out, mlstm_w_in, mlstm_gate_b, mlstm_norm_g, mlstm_w_out, r7_mu, r7_w_rkvz, r7_w0, r7_w1, r7_w2, r7_a0, r7_a1, r7_a2, r7_k_k, r7_k_a, r7_r_k, r7_ln_g, r7_ln_b, r7_w_out):
    bsz, t, d = x.shape
    ctx_len = ctx.shape[1]
    depth = norm_g.shape[0]
    assert ctx_len % TM == 0 and t % TB == 0 and TB % TM == 0 and t % GRID_W == 0 and bsz < SUBLANES
    nx = t // TM

    cc = jnp.zeros((SUBLANES, d), F32).at[:bsz].set(c).at[bsz].set(c_ctx)
    mp = _mod_params(cc, mod_w, mod_b).reshape(depth, SUBLANES, 3, d)
    mods_all = jnp.stack([jnp.broadcast_to(mp[:, bsz][:, None], (depth, bsz, 3, d)), mp[:, :bsz]],
                         axis=2).reshape(depth, 2 * bsz, 3, d)

    xs = jnp.concatenate([x, ctx], axis=1)
    for i in range(depth):
        kind, j = i % 3, i // 3
        mods = mods_all[i]
        if kind == 0:
            xs = _lru_layer(xs, mods, norm_g[i], lru_w_in[j], lru_conv_w[j], lru_conv_b[j],
                            lru_gate_w[j], lru_gate_b[j], lru_lam[j], lru_w_out[j], nx,
                            final_g=final_g if i == depth - 1 else None)
        elif kind == 1:
            xs = _mlstm_layer(xs, mods, norm_g[i], mlstm_w_in[j], mlstm_gate_b[j], mlstm_norm_g[j],
                              mlstm_w_out[j], nx)
        else:
            xs = _rwkv_layer(xs, mods, norm_g[i], r7_mu[j], r7_w_rkvz[j], r7_w0[j], r7_w1[j], r7_w2[j],
                             r7_a0[j], r7_a1[j], r7_a2[j], r7_k_k[j], r7_k_a[j], r7_r_k[j],
                             r7_ln_g[j], r7_ln_b[j], r7_w_out[j], nx)
    if depth % 3 == 1:
        return xs
    return _final_norm(xs, final_g, t)
```
